```python
import math
import jax, jax.numpy as jnp
from jax import lax
import numpy as np

D_MODEL = 1024
BATCH = 8
SEQ = 2048
DEPTH = 1

MLA_HEADS = 8
MLA_QK_NOPE = 64
MLA_QK_ROPE = 32
MLA_V_DIM = 64
MLA_Q_RANK = 384
MLA_KV_RANK = 256
DIFF_HEADS = 4
DIFF_HALF = 64
DIFF_V_DIM = 2 * DIFF_HALF
MIX_WIDTH = MLA_HEADS * MLA_V_DIM + DIFF_HEADS * DIFF_V_DIM
SPLIT_SIZES = (MLA_Q_RANK, MLA_KV_RANK, MLA_QK_ROPE,
               DIFF_HEADS * DIFF_V_DIM, DIFF_HEADS * DIFF_V_DIM, DIFF_HEADS * DIFF_V_DIM)
SPLIT_POINTS = tuple(int(v) for v in np.cumsum(SPLIT_SIZES)[:-1])
IN_COLS = int(sum(SPLIT_SIZES))
N_GROUPS = 4
EXPERTS_PER_GROUP = 8
N_EXPERTS = N_GROUPS * EXPERTS_PER_GROUP
TOP_K_INNER = 2
D_FF_EXPERT = 256

ROPE_THETA = 10000.0
Q_BLOCK = 128
EPS = 1e-6

kernel_name = "hybrid_mla_diffattn_hier_moe"


def rmsnorm(x, g):
    xf = x.astype(jnp.float32)
    y = xf * lax.rsqrt(jnp.mean(xf * xf, axis=-1, keepdims=True) + EPS)
    return (y * g.astype(jnp.float32)).astype(x.dtype)


def rope(x, positions):
    half = x.shape[-1] // 2
    inv_freq = ROPE_THETA ** (-jnp.arange(half, dtype=jnp.float32) / half)
    ang = positions.astype(jnp.float32)[:, :, None, None] * inv_freq
    cos, sin = jnp.cos(ang), jnp.sin(ang)
    x1 = x[..., :half].astype(jnp.float32)
    x2 = x[..., half:].astype(jnp.float32)
    out = jnp.concatenate([x1 * cos - x2 * sin, x2 * cos + x1 * sin], axis=-1)
    return out.astype(x.dtype)


def causal_attention(q, k, v, scale):
    B, S, H, Dk = q.shape
    Dv = v.shape[-1]
    nb = S // Q_BLOCK
    qb = q.reshape(B, nb, Q_BLOCK, H, Dk).transpose(1, 0, 2, 3, 4)
    kf = k.astype(jnp.float32)
    vf = v.astype(jnp.float32)
    kpos = jnp.arange(S)

    def one_block(args):
        q_blk, i = args
        s = jnp.einsum('bqhd,bkhd->bhqk', q_blk.astype(jnp.float32), kf) * scale
        qpos = i * Q_BLOCK + jnp.arange(Q_BLOCK)
        mask = kpos[None, :] <= qpos[:, None]
        s = jnp.where(mask[None, None], s, jnp.finfo(jnp.float32).min)
        p = jax.nn.softmax(s, axis=-1)
        return jnp.einsum('bhqk,bkhd->bqhd', p, vf)

    o = lax.map(one_block, (qb, jnp.arange(nb)))
    return o.transpose(1, 0, 2, 3, 4).reshape(B, S, H, Dv).astype(v.dtype)


def hybrid_mixer(h, positions, w_in, q_norm_g, w_uq, kv_norm_g, w_ukv,
                 lam_q1, lam_k1, lam_q2, lam_k2, subln_g, w_o, lam_init):
    B, S, _ = h.shape
    proj = h @ w_in
    c_q, c_kv, k_pe, dq, dk, dv = jnp.split(proj, SPLIT_POINTS, axis=-1)

    q = (rmsnorm(c_q, q_norm_g) @ w_uq).reshape(B, S, MLA_HEADS, MLA_QK_NOPE + MLA_QK_ROPE)
    q_nope, q_pe = q[..., :MLA_QK_NOPE], q[..., MLA_QK_NOPE:]
    kv = (rmsnorm(c_kv, kv_norm_g) @ w_ukv).reshape(B, S, MLA_HEADS, MLA_QK_NOPE + MLA_V_DIM)
    k_nope, v_mla = kv[..., :MLA_QK_NOPE], kv[..., MLA_QK_NOPE:]
    k_pe = rope(k_pe[:, :, None, :], positions)
    q_m = jnp.concatenate([q_nope, rope(q_pe, positions)], axis=-1)
    k_m = jnp.concatenate([k_nope, jnp.broadcast_to(k_pe, (B, S, MLA_HEADS, MLA_QK_ROPE))], axis=-1)
    o_mla = causal_attention(q_m, k_m, v_mla, (MLA_QK_NOPE + MLA_QK_ROPE) ** -0.5)
    o_mla = o_mla.reshape(B, S, MLA_HEADS * MLA_V_DIM)

    dq = rope(dq.reshape(B, S, 2 * DIFF_HEADS, DIFF_HALF), positions).reshape(B, S, DIFF_HEADS, 2, DIFF_HALF)
    dk = rope(dk.reshape(B, S, 2 * DIFF_HEADS, DIFF_HALF), positions).reshape(B, S, DIFF_HEADS, 2, DIFF_HALF)
    dv = dv.reshape(B, S, DIFF_HEADS, DIFF_V_DIM)
    lam = (jnp.exp(jnp.sum(lam_q1.astype(jnp.float32) * lam_k1.astype(jnp.float32)))
           - jnp.exp(jnp.sum(lam_q2.astype(jnp.float32) * lam_k2.astype(jnp.float32)))
           + lam_init)
    scale_d = DIFF_HALF ** -0.5
    a1 = causal_attention(dq[..., 0, :], dk[..., 0, :], dv, scale_d).astype(jnp.float32)
    a2 = causal_attention(dq[..., 1, :], dk[..., 1, :], dv, scale_d).astype(jnp.float32)
    o_diff = rmsnorm(a1 - lam * a2, subln_g) * (1.0 - lam_init)
    o_diff = o_diff.astype(h.dtype).reshape(B, S, DIFF_HEADS * DIFF_V_DIM)

    return jnp.concatenate([o_mla, o_diff], axis=-1) @ w_o


def hier_moe(h, w_rg, b_rg, w_re, b_re, w_gate, w_up, w_down):
    B, S, D = h.shape
    t = h.reshape(B * S, D)
    g_logits = (t @ w_rg).astype(jnp.float32) + b_rg.astype(jnp.float32)
    p_group = jax.nn.softmax(g_logits, axis=-1)
    g_idx = jnp.argmax(g_logits, axis=-1)
    g_onehot = jax.nn.one_hot(g_idx, N_GROUPS, dtype=jnp.float32)
    p_g_sel = jnp.sum(p_group * g_onehot, axis=-1, keepdims=True)
    e_logits = (jnp.einsum('td,dge->tge', t, w_re).astype(jnp.float32)
                + b_re.astype(jnp.float32))
    e_logits = jnp.einsum('tge,tg->te', e_logits, g_onehot)
    p_exp = jax.nn.softmax(e_logits, axis=-1)
    top_p, top_i = lax.top_k(p_exp, TOP_K_INNER)
    top_p = top_p / jnp.sum(top_p, axis=-1, keepdims=True)
    expert_id = g_idx[:, None] * EXPERTS_PER_GROUP + top_i
    gates = jnp.einsum('tk,tkn->tn', p_g_sel * top_p,
                       jax.nn.one_hot(expert_id, N_EXPERTS, dtype=jnp.float32))
    a = jnp.einsum('td,ndf->tnf', t, w_gate)
    u = jnp.einsum('td,ndf->tnf', t, w_up)
    hdn = jax.nn.silu(a) * u * gates[:, :, None].astype(t.dtype)
    y = jnp.einsum('tnf,nfd->td', hdn, w_down)
    return y.reshape(B, S, D)


def setup_inputs(seed: int = 0) -> dict:
    key = jax.random.key(seed)
    ks = jax.random.split(key, 24)
    f32 = jnp.float32

    def nrm(k, shape, fan_in):
        return jax.random.normal(k, shape, f32) * (fan_in ** -0.5)

    def gain(k, shape):
        return 1.0 + 0.05 * jax.random.normal(k, shape, f32)

    L = DEPTH
    return {
        "x": jax.random.normal(ks[0], (BATCH, SEQ, D_MODEL), f32),
        "positions": jnp.broadcast_to(jnp.arange(SEQ, dtype=jnp.int32)[None, :], (BATCH, SEQ)),
        "attn_norm_g": gain(ks[1], (L, D_MODEL)),
        "w_in": nrm(ks[2], (L, D_MODEL, IN_COLS), D_MODEL),
        "q_norm_g": gain(ks[3], (L, MLA_Q_RANK)),
        "w_uq": nrm(ks[4], (L, MLA_Q_RANK, MLA_HEADS * (MLA_QK_NOPE + MLA_QK_ROPE)), MLA_Q_RANK),
        "kv_norm_g": gain(ks[5], (L, MLA_KV_RANK)),
        "w_ukv": nrm(ks[6], (L, MLA_KV_RANK, MLA_HEADS * (MLA_QK_NOPE + MLA_V_DIM)), MLA_KV_RANK),
        "lambda_q1": 0.1 * jax.random.normal(ks[7], (L, DIFF_HALF), f32),
        "lambda_k1": 0.1 * jax.random.normal(ks[8], (L, DIFF_HALF), f32),
        "lambda_q2": 0.1 * jax.random.normal(ks[9], (L, DIFF_HALF), f32),
        "lambda_k2": 0.1 * jax.random.normal(ks[10], (L, DIFF_HALF), f32),
        "subln_g": gain(ks[11], (L, DIFF_V_DIM)),
        "w_o": nrm(ks[12], (L, MIX_WIDTH, D_MODEL), MIX_WIDTH),
        "ffn_norm_g": gain(ks[13], (L, D_MODEL)),
        "w_router_group": nrm(ks[14], (L, D_MODEL, N_GROUPS), D_MODEL),
        "b_router_group": 0.01 * jax.random.normal(ks[15], (L, N_GROUPS), f32),
        "w_router_expert": nrm(ks[16], (L, D_MODEL, N_GROUPS, EXPERTS_PER_GROUP), D_MODEL),
        "b_router_expert": 0.01 * jax.random.normal(ks[17], (L, N_GROUPS, EXPERTS_PER_GROUP), f32),
        "w_gate": nrm(ks[18], (L, N_EXPERTS, D_MODEL, D_FF_EXPERT), D_MODEL),
        "w_up": nrm(ks[19], (L, N_EXPERTS, D_MODEL, D_FF_EXPERT), D_MODEL),
        "w_down": nrm(ks[20], (L, N_EXPERTS, D_FF_EXPERT, D_MODEL), D_FF_EXPERT),
        "final_norm_g": gain(ks[21], (D_MODEL,)),
    }


def reference(x, positions, attn_norm_g, w_in, q_norm_g, w_uq, kv_norm_g, w_ukv,
              lambda_q1, lambda_k1, lambda_q2, lambda_k2, subln_g, w_o,
              ffn_norm_g, w_router_group, b_router_group, w_router_expert, b_router_expert,
              w_gate, w_up, w_down, final_norm_g):
    h = x
    for l in range(DEPTH):
        lam_init = 0.8 - 0.6 * math.exp(-0.3 * l)
        mixed = hybrid_mixer(rmsnorm(h, attn_norm_g[l]), positions, w_in[l], q_norm_g[l], w_uq[l],
                             kv_norm_g[l], w_ukv[l], lambda_q1[l], lambda_k1[l], lambda_q2[l],
                             lambda_k2[l], subln_g[l], w_o[l], lam_init)
        h = h + mixed.astype(h.dtype)
        moe = hier_moe(rmsnorm(h, ffn_norm_g[l]), w_router_group[l], b_router_group[l],
                       w_router_expert[l], b_router_expert[l], w_gate[l], w_up[l], w_down[l])
        h = h + moe.astype(h.dtype)
    return rmsnorm(h, final_norm_g)
```

```python
import functools
import math

import jax
import jax.numpy as jnp
from jax import lax
from jax.experimental import pallas as pl
from jax.experimental.pallas import tpu as pltpu

F32 = jnp.float32
BF16 = jnp.bfloat16

D_MODEL = 1024
MLA_HEADS = 8
MLA_QK_NOPE = 64
MLA_QK_ROPE = 32
MLA_V_DIM = 64
MLA_Q_RANK = 384
MLA_KV_RANK = 256
DIFF_HEADS = 4
DIFF_HALF = 64
DIFF_V_DIM = 2 * DIFF_HALF
N_GROUPS = 4
EXPERTS_PER_GROUP = 8
N_EXPERTS = N_GROUPS * EXPERTS_PER_GROUP
D_FF_EXPERT = 256
ROPE_THETA = 10000.0
EPS = 1e-6
LAM_INIT = 0.8 - 0.6 * math.exp(-0.3 * 0)

LANES = 128
N_QK_BLOCKS = MLA_HEADS + DIFF_HEADS
N_V_BLOCKS = MLA_HEADS // 2 + DIFF_HEADS
N_UNITS = MLA_HEADS // 2 + DIFF_HEADS
PROJ_COLS = MLA_Q_RANK + MLA_KV_RANK + LANES + 3 * DIFF_HEADS * DIFF_V_DIM
LOG2E = 1.4426950408889634
NEG_BIG = -1e30

TM_PROJ = 512
TQ = 256
TK = 256
TM_POST = 512
TM_MOE = 256
TM_FIN = 256
VMEM_LIMIT = 48 * 1024 * 1024


def _rms(x, g):
    return x * lax.rsqrt(jnp.mean(x * x, axis=-1, keepdims=True) + EPS) * g


def _proj_kernel(x_ref, pos_ref, freq_ref, g_attn_ref, w_in_ref, gq_ref, w_uq_ref, gkv_ref, w_ukv_ref,
                 qa_ref, ka_ref, va_ref):
    xn = _rms(x_ref[0], g_attn_ref[...]).astype(BF16)
    proj = jnp.dot(xn, w_in_ref[...], preferred_element_type=F32)

    ang = pos_ref[0] * freq_ref[...]
    cos_t, sin_t = jnp.cos(ang), jnp.sin(ang)
    lane = lax.broadcasted_iota(jnp.int32, ang.shape, 1)
    in_rope = (lane >= 64) & (lane < 96)
    cos_a = jnp.where(in_rope, cos_t, 1.0)
    sin_a_hi = jnp.where((lane >= 80) & (lane < 96), sin_t, 0.0)
    sin_a_lo = jnp.where((lane >= 64) & (lane < 80), -sin_t, 0.0)
    cos_b = jnp.where(lane < 64, cos_t, pltpu.roll(cos_t, 64, 1))
    sin_b = jnp.where(lane < 64, sin_t, pltpu.roll(sin_t, 64, 1))
    sub = lane % 64
    sin_b_hi = jnp.where(sub >= 32, sin_b, 0.0)
    sin_b_lo = jnp.where(sub < 32, -sin_b, 0.0)

    def rope_a(xb, scale):
        return (xb * (cos_a * scale) + pltpu.roll(xb, 16, 1) * (sin_a_hi * scale)
                + pltpu.roll(xb, 112, 1) * (sin_a_lo * scale))

    def rope_b(xb, scale):
        return (xb * (cos_b * scale) + pltpu.roll(xb, 32, 1) * (sin_b_hi * scale)
                + pltpu.roll(xb, 96, 1) * (sin_b_lo * scale))

    c_q = proj[:, :MLA_Q_RANK]
    c_kv = proj[:, MLA_Q_RANK:MLA_Q_RANK + MLA_KV_RANK]
    base = MLA_Q_RANK + MLA_KV_RANK
    kpe = rope_a(proj[:, base:base + LANES], 1.0)
    base += LANES

    q = jnp.dot(_rms(c_q, gq_ref[...]).astype(BF16), w_uq_ref[...], preferred_element_type=F32)
    kv = jnp.dot(_rms(c_kv, gkv_ref[...]).astype(BF16), w_ukv_ref[...], preferred_element_type=F32)

    scale_mla = (MLA_QK_NOPE + MLA_QK_ROPE) ** -0.5 * LOG2E
    scale_diff = DIFF_HALF ** -0.5 * LOG2E
    for h in range(MLA_HEADS):
        sl = slice(h * LANES, (h + 1) * LANES)
        qa_ref[0, h] = rope_a(q[:, sl], scale_mla).astype(BF16)
        ka_ref[0, h] = (kv[:, sl] + kpe).astype(BF16)
    vbase = MLA_HEADS * LANES
    for hp in range(MLA_HEADS // 2):
        va_ref[0, hp] = kv[:, vbase + hp * LANES: vbase + (hp + 1) * LANES].astype(BF16)
    width = DIFF_HEADS * DIFF_V_DIM
    for h in range(DIFF_HEADS):
        sl = slice(base + h * LANES, base + (h + 1) * LANES)
        qa_ref[0, MLA_HEADS + h] = rope_b(proj[:, sl], scale_diff).astype(BF16)
        sl = slice(base + width + h * LANES, base + width + (h + 1) * LANES)
        ka_ref[0, MLA_HEADS + h] = rope_b(proj[:, sl], 1.0).astype(BF16)
        sl = slice(base + 2 * width + h * LANES, base + 2 * width + (h + 1) * LANES)
        va_ref[0, MLA_HEADS // 2 + h] = proj[:, sl].astype(BF16)


def _run_proj(x, pos_f, freq, g_attn, w_in_r, gq, w_uq_r, gkv, w_ukv_r):
    B, S, _ = x.shape
    tm = TM_PROJ
    full = lambda shape: pl.BlockSpec(shape, lambda b, i: (0,) * len(shape))
    head_out = lambda n: pl.BlockSpec((1, n, tm, LANES), lambda b, i: (b, 0, i, 0))
    return pl.pallas_call(
        _proj_kernel,
        grid=(B, S // tm),
        in_specs=[
            pl.BlockSpec((1, tm, D_MODEL), lambda b, i: (b, i, 0)),
            pl.BlockSpec((1, tm, 1), lambda b, i: (b, i, 0)),
            full((1, LANES)),
            full((1, D_MODEL)),
            full((D_MODEL, PROJ_COLS)),
            full((1, MLA_Q_RANK)),
            full((MLA_Q_RANK, MLA_HEADS * LANES)),
            full((1, MLA_KV_RANK)),
            full((MLA_KV_RANK, MLA_HEADS * LANES + MLA_HEADS * MLA_V_DIM)),
        ],
        out_specs=[head_out(N_QK_BLOCKS), head_out(N_QK_BLOCKS), head_out(N_V_BLOCKS)],
        out_shape=[
            jax.ShapeDtypeStruct((B, N_QK_BLOCKS, S, LANES), BF16),
            jax.ShapeDtypeStruct((B, N_QK_BLOCKS, S, LANES), BF16),
            jax.ShapeDtypeStruct((B, N_V_BLOCKS, S, LANES), BF16),
        ],
        compiler_params=pltpu.CompilerParams(
            dimension_semantics=("arbitrary", "arbitrary"), vmem_limit_bytes=VMEM_LIMIT),
        name="proj",
    )(x, pos_f, freq, g_attn, w_in_r, gq, w_uq_r, gkv, w_ukv_r)


def _attn_kernel(qa_ref, ka_ref, va_ref, lam_ref, subln_ref, o_ref, m_ref, l_ref, acc_ref):
    qi = pl.program_id(1)
    m_ref[...] = jnp.full(m_ref.shape, NEG_BIG, F32)
    l_ref[...] = jnp.zeros(l_ref.shape, F32)
    acc_ref[...] = jnp.zeros(acc_ref.shape, F32)

    lane = lax.broadcasted_iota(jnp.int32, (TQ, LANES), 1)
    low_half = lane < 64

    def unit_operands(u):
        if u < MLA_HEADS // 2:
            return (qa_ref[0, 2 * u], 2 * u), (qa_ref[0, 2 * u + 1], 2 * u + 1), u
        h = u - MLA_HEADS // 2
        q = qa_ref[0, MLA_HEADS + h]
        zero = jnp.zeros_like(q)
        return ((jnp.where(low_half, q, zero), MLA_HEADS + h),
                (jnp.where(low_half, zero, q), MLA_HEADS + h), MLA_HEADS // 2 + h)

    def stream_step(slot, q, k, v, mask):
        s = lax.dot_general(q, k, (((1,), (1,)), ((), ())), preferred_element_type=F32)
        if mask is not None:
            s = jnp.where(mask, s, NEG_BIG)
        m_prev = m_ref[slot]
        m_new = jnp.maximum(m_prev, jnp.max(s, axis=1, keepdims=True))
        alpha = jnp.exp2(m_prev - m_new)
        p = jnp.exp2(s - jnp.concatenate([m_new] * (TK // LANES), axis=1))
        l_ref[slot] = alpha * l_ref[slot] + jnp.sum(p, axis=1, keepdims=True)
        acc_ref[slot] = alpha * acc_ref[slot] + jnp.dot(p.astype(BF16), v, preferred_element_type=F32)
        m_ref[slot] = m_new

    def kv_step(j, mask):
        rows = pl.ds(pl.multiple_of(j * TK, TK), TK)
        for u in range(N_UNITS):
            (q0, k0), (q1, k1), vi = unit_operands(u)
            v = va_ref[0, vi, rows, :]
            stream_step(2 * u, q0, ka_ref[0, k0, rows, :], v, mask)
            stream_step(2 * u + 1, q1, ka_ref[0, k1, rows, :], v, mask)

    def body(j, carry):
        kv_step(j, None)
        return carry

    lax.fori_loop(0, qi, body, 0)
    row = lax.broadcasted_iota(jnp.int32, (TQ, TK), 0)
    col = lax.broadcasted_iota(jnp.int32, (TQ, TK), 1)
    kv_step(qi, col <= row)

    lam_v = lam_ref[...]
    lam = (jnp.exp(jnp.sum(lam_v[0:1] * lam_v[1:2], axis=1, keepdims=True))
           - jnp.exp(jnp.sum(lam_v[2:3] * lam_v[3:4], axis=1, keepdims=True)) + LAM_INIT)
    for u in range(N_UNITS):
        o0 = acc_ref[2 * u] / l_ref[2 * u]
        o1 = acc_ref[2 * u + 1] / l_ref[2 * u + 1]
        if u < MLA_HEADS // 2:
            out = jnp.where(low_half, o0, o1)
            col0 = u * LANES
        else:
            out = _rms(o0 - lam * o1, subln_ref[...]) * (1.0 - LAM_INIT)
            col0 = MLA_HEADS * MLA_V_DIM + (u - MLA_HEADS // 2) * LANES
        o_ref[0, :, col0:col0 + LANES] = out.astype(BF16)


def _run_attn(qa, ka, va, lam_vecs, subln):
    B, _, S, _ = qa.shape
    n_slots = 2 * N_UNITS
    return pl.pallas_call(
        _attn_kernel,
        grid=(B, S // TQ),
        in_specs=[
            pl.BlockSpec((1, N_QK_BLOCKS, TQ, LANES), lambda b, i: (b, 0, i, 0)),
            pl.BlockSpec((1, N_QK_BLOCKS, S, LANES), lambda b, i: (b, 0, 0, 0)),
            pl.BlockSpec((1, N_V_BLOCKS, S, LANES), lambda b, i: (b, 0, 0, 0)),
            pl.BlockSpec((4, DIFF_HALF), lambda b, i: (0, 0)),
            pl.BlockSpec((1, DIFF_V_DIM), lambda b, i: (0, 0)),
        ],
        out_specs=pl.BlockSpec((1, TQ, D_MODEL), lambda b, i: (b, i, 0)),
        out_shape=jax.ShapeDtypeStruct((B, S, D_MODEL), BF16),
        scratch_shapes=[
            pltpu.VMEM((n_slots, TQ, LANES), F32),
            pltpu.VMEM((n_slots, TQ, LANES), F32),
            pltpu.VMEM((n_slots, TQ, LANES), F32),
        ],
        compiler_params=pltpu.CompilerParams(
            dimension_semantics=("arbitrary", "arbitrary"), vmem_limit_bytes=VMEM_LIMIT),
        name="attn",
    )(qa, ka, va, lam_vecs, subln)


def _post_kernel(o_ref, x_ref, w_o_ref, g_ffn_ref, w_r_ref, b_r_ref, tri_ref,
                 h_ref, t_ref, route_ref, cnt_ref):
    @pl.when(pl.program_id(0) == 0)
    def _():
        cnt_ref[...] = jnp.zeros(cnt_ref.shape, F32)

    h = x_ref[...] + jnp.dot(o_ref[...], w_o_ref[...], preferred_element_type=F32)
    h_ref[...] = h
    t = _rms(h, g_ffn_ref[...])
    t_ref[...] = t
    lg = jnp.dot(t.astype(BF16), w_r_ref[...], preferred_element_type=F32) + b_r_ref[...]

    lane = lax.broadcasted_iota(jnp.int32, lg.shape, 1)
    lane_f = lane.astype(F32)
    first_lane = lambda hit: jnp.min(jnp.where(hit, lane_f, float(LANES)), axis=1, keepdims=True)

    is_group = lane < N_GROUPS
    gl = jnp.where(is_group, lg, NEG_BIG)
    gmax = jnp.max(gl, axis=1, keepdims=True)
    gidx = first_lane(gl == gmax)
    p_group = 1.0 / jnp.sum(jnp.where(is_group, jnp.exp(gl - gmax), 0.0), axis=1, keepdims=True)

    lo = N_GROUPS + EXPERTS_PER_GROUP * gidx
    el = jnp.where((lane_f >= lo) & (lane_f < lo + EXPERTS_PER_GROUP), lg, NEG_BIG)
    m1 = jnp.max(el, axis=1, keepdims=True)
    i1 = first_lane(el == m1)
    el2 = jnp.where(lane_f == i1, NEG_BIG, el)
    m2 = jnp.max(el2, axis=1, keepdims=True)
    i2 = first_lane(el2 == m2)
    ratio = jnp.exp(m2 - m1)
    w1 = p_group / (1.0 + ratio)
    w2 = p_group * ratio / (1.0 + ratio)
    e1 = i1 - N_GROUPS
    e2 = i2 - N_GROUPS

    hit1 = lane_f == e1
    hit2 = lane_f == e2
    both = jnp.where(hit1 | hit2, 1.0, 0.0)
    before = jnp.dot(tri_ref[...], both.astype(BF16), preferred_element_type=F32) + cnt_ref[...]
    r1 = jnp.sum(jnp.where(hit1, before, 0.0), axis=1, keepdims=True)
    r2 = jnp.sum(jnp.where(hit2, before, 0.0), axis=1, keepdims=True)
    cnt_ref[...] += jnp.sum(both, axis=0, keepdims=True)

    out = jnp.zeros(lg.shape, F32)
    for k, val in enumerate((e1, e2, w1, w2, r1, r2)):
        out = jnp.where(lane == k, val, out)
    route_ref[...] = out


def _run_post(o2, x2, w_o, g_ffn, w_r, b_r, tri):
    T = x2.shape[0]
    tm = TM_POST
    full = lambda shape: pl.BlockSpec(shape, lambda i: (0,) * len(shape))
    rows = lambda w: pl.BlockSpec((tm, w), lambda i: (i, 0))
    return pl.pallas_call(
        _post_kernel,
        grid=(T // tm,),
        in_specs=[rows(D_MODEL), rows(D_MODEL), full((D_MODEL, D_MODEL)), full((1, D_MODEL)),
                  full((D_MODEL, LANES)), full((1, LANES)), full((tm, tm))],
        out_specs=[rows(D_MODEL), rows(D_MODEL), rows(LANES), full((1, LANES))],
        out_shape=[
            jax.ShapeDtypeStruct((T, D_MODEL), F32),
            jax.ShapeDtypeStruct((T, D_MODEL), F32),
            jax.ShapeDtypeStruct((T, LANES), F32),
            jax.ShapeDtypeStruct((1, LANES), F32),
        ],
        compiler_params=pltpu.CompilerParams(
            dimension_semantics=("arbitrary",), vmem_limit_bytes=VMEM_LIMIT),
        name="post",
    )(o2, x2, w_o, g_ffn, w_r, b_r, tri)


def _moe_kernel(te_ref, tv_ref, src_ref, gate_ref, t_hbm, wg_ref, wu_ref, wd_ref,
                y_ref, xbuf, sem):
    i = pl.program_id(0)

    @pl.when(tv_ref[i] == 0)
    def _():
        y_ref[...] = jnp.zeros(y_ref.shape, F32)

    def row_copy(r):
        return pltpu.make_async_copy(t_hbm.at[pl.ds(src_ref[0, 0, r], 1), :],
                                     xbuf.at[pl.ds(r, 1), :], sem)

    @pl.when(tv_ref[i] > 0)
    def _():
        def start(r, c):
            row_copy(r).start()
            return c

        def wait(r, c):
            row_copy(r).wait()
            return c

        lax.fori_loop(0, TM_MOE, start, 0)
        lax.fori_loop(0, TM_MOE, wait, 0)
        x = xbuf[...].astype(BF16)
        a = jnp.dot(x, wg_ref[0], preferred_element_type=F32)
        u = jnp.dot(x, wu_ref[0], preferred_element_type=F32)
        hdn = (a * jax.nn.sigmoid(a)) * u * gate_ref[...]
        y_ref[...] = jnp.dot(hdn.astype(BF16), wd_ref[0], preferred_element_type=F32)


def _run_moe(tile_expert, tile_valid, src3, gate_s, t2, wg, wu, wd):
    n_tiles = src3.shape[0]
    grid_spec = pltpu.PrefetchScalarGridSpec(
        num_scalar_prefetch=2,
        grid=(n_tiles,),
        in_specs=[
            pl.BlockSpec((1, 1, TM_MOE), lambda i, te, tv: (i, 0, 0), memory_space=pltpu.SMEM),
            pl.BlockSpec((TM_MOE, 1), lambda i, te, tv: (i, 0)),
            pl.BlockSpec(memory_space=pl.ANY),
            pl.BlockSpec((1, D_MODEL, D_FF_EXPERT), lambda i, te, tv: (te[i], 0, 0)),
            pl.BlockSpec((1, D_MODEL, D_FF_EXPERT), lambda i, te, tv: (te[i], 0, 0)),
            pl.BlockSpec((1, D_FF_EXPERT, D_MODEL), lambda i, te, tv: (te[i], 0, 0)),
        ],
        out_specs=pl.BlockSpec((TM_MOE, D_MODEL), lambda i, te, tv: (i, 0)),
        scratch_shapes=[pltpu.VMEM((TM_MOE, D_MODEL), F32), pltpu.SemaphoreType.DMA(())],
    )
    return pl.pallas_call(
        _moe_kernel,
        grid_spec=grid_spec,
        out_shape=jax.ShapeDtypeStruct((n_tiles * TM_MOE, D_MODEL), F32),
        compiler_params=pltpu.CompilerParams(
            dimension_semantics=("arbitrary",), vmem_limit_bytes=VMEM_LIMIT),
        name="moe",
    )(tile_expert, tile_valid, src3, gate_s, t2, wg, wu, wd)


def _final_kernel(pos_ref, h_ref, y_hbm, g_ref, out_ref, ybuf, sem):
    def row_copy(r):
        return pltpu.make_async_copy(y_hbm.at[pl.ds(pos_ref[0, 0, r], 1), :],
                                     ybuf.at[pl.ds(r, 1), :], sem)

    def start(r, c):
        row_copy(r).start()
        return c

    def wait(r, c):
        row_copy(r).wait()
        return c

    lax.fori_loop(0, 2 * TM_FIN, start, 0)
    lax.fori_loop(0, 2 * TM_FIN, wait, 0)
    h = h_ref[...] + ybuf[0:TM_FIN, :] + ybuf[TM_FIN:2 * TM_FIN, :]
    out_ref[...] = _rms(h, g_ref[...])


def _run_final(pos3, h2, ys, g_final):
    T = h2.shape[0]
    return pl.pallas_call(
        _final_kernel,
        grid=(T // TM_FIN,),
        in_specs=[
            pl.BlockSpec((1, 1, 2 * TM_FIN), lambda i: (i, 0, 0), memory_space=pltpu.SMEM),
            pl.BlockSpec((TM_FIN, D_MODEL), lambda i: (i, 0)),
            pl.BlockSpec(memory_space=pl.ANY),
            pl.BlockSpec((1, D_MODEL), lambda i: (0, 0)),
        ],
        out_specs=pl.BlockSpec((TM_FIN, D_MODEL), lambda i: (i, 0)),
        out_shape=jax.ShapeDtypeStruct((T, D_MODEL), F32),
        scratch_shapes=[pltpu.VMEM((2 * TM_FIN, D_MODEL), F32), pltpu.SemaphoreType.DMA(())],
        compiler_params=pltpu.CompilerParams(
            dimension_semantics=("arbitrary",), vmem_limit_bytes=VMEM_LIMIT),
        name="final",
    )(pos3, h2, ys, g_final)


def _prep_attention_weights(w_in, w_uq, w_ukv):
    d = w_in.shape[0]
    c0 = MLA_Q_RANK + MLA_KV_RANK
    kpe = w_in[:, c0:c0 + MLA_QK_ROPE]
    kpe_blk = jnp.concatenate(
        [jnp.zeros((d, MLA_QK_NOPE), F32), kpe, jnp.zeros((d, LANES - MLA_QK_NOPE - MLA_QK_ROPE), F32)], axis=1)
    w_in_r = jnp.concatenate([w_in[:, :c0], kpe_blk, w_in[:, c0 + MLA_QK_ROPE:]], axis=1).astype(BF16)

    dqk = MLA_QK_NOPE + MLA_QK_ROPE
    uq = w_uq.reshape(MLA_Q_RANK, MLA_HEADS, dqk)
    uq = jnp.pad(uq, ((0, 0), (0, 0), (0, LANES - dqk))).reshape(MLA_Q_RANK, MLA_HEADS * LANES).astype(BF16)

    ukv = w_ukv.reshape(MLA_KV_RANK, MLA_HEADS, MLA_QK_NOPE + MLA_V_DIM)
    uk = jnp.pad(ukv[:, :, :MLA_QK_NOPE], ((0, 0), (0, 0), (0, LANES - MLA_QK_NOPE)))
    uk = uk.reshape(MLA_KV_RANK, MLA_HEADS * LANES)
    uv = ukv[:, :, MLA_QK_NOPE:].reshape(MLA_KV_RANK, MLA_HEADS * MLA_V_DIM)
    w_ukv_r = jnp.concatenate([uk, uv], axis=1).astype(BF16)
    return w_in_r, uq, w_ukv_r


def _rope_freq_lanes():
    fb = ROPE_THETA ** (-jnp.arange(DIFF_HALF // 2, dtype=F32) / (DIFF_HALF // 2))
    fa = ROPE_THETA ** (-jnp.arange(MLA_QK_ROPE // 2, dtype=F32) / (MLA_QK_ROPE // 2))
    return jnp.concatenate([fb, fb, fa, fa, jnp.zeros((LANES - 96,), F32)])[None, :]


def kernel(x, positions, attn_norm_g, w_in, q_norm_g, w_uq, kv_norm_g, w_ukv, lambda_q1, lambda_k1, lambda_q2, lambda_k2, subln_g, w_o, ffn_norm_g, w_router_group, b_router_group, w_router_expert, b_router_expert, w_gate, w_up, w_down, final_norm_g):
    B, S, D = x.shape
    T = B * S
    l = 0

    w_in_r, w_uq_r, w_ukv_r = _prep_attention_weights(w_in[l], w_uq[l], w_ukv[l])
    pos_f = positions.astype(F32)[:, :, None]
    qa, ka, va = _run_proj(x, pos_f, _rope_freq_lanes(), attn_norm_g[l][None, :], w_in_r,
                           q_norm_g[l][None, :], w_uq_r, kv_norm_g[l][None, :], w_ukv_r)

    lam_vecs = jnp.stack([lambda_q1[l], lambda_k1[l], lambda_q2[l], lambda_k2[l]]).astype(F32)
    o = _run_attn(qa, ka, va, lam_vecs, subln_g[l][None, :])

    w_r = jnp.concatenate(
        [w_router_group[l], w_router_expert[l].reshape(D, N_EXPERTS),
         jnp.zeros((D, LANES - N_GROUPS - N_EXPERTS), F32)], axis=1).astype(BF16)
    b_r = jnp.concatenate(
        [b_router_group[l], b_router_expert[l].reshape(N_EXPERTS),
         jnp.zeros((LANES - N_GROUPS - N_EXPERTS,), F32)])[None, :].astype(F32)
    ii = jnp.arange(TM_POST)
    tri = (ii[None, :] < ii[:, None]).astype(BF16)
    x2 = x.reshape(T, D)
    h2, t2, route, counts = _run_post(o.reshape(T, D), x2, w_o[l].astype(BF16), ffn_norm_g[l][None, :],
                                      w_r, b_r, tri)

    n_tiles_max = (2 * T) // TM_MOE + N_EXPERTS
    eid = route[:, 0:2].astype(jnp.int32)
    gates = route[:, 2:4]
    rank = route[:, 4:6].astype(jnp.int32)
    cnt = counts[0, :N_EXPERTS].astype(jnp.int32)
    tiles_e = (cnt + TM_MOE - 1) // TM_MOE
    tile_end = jnp.cumsum(tiles_e)
    row_off = (tile_end - tiles_e) * TM_MOE
    pos = row_off[eid] + rank
    n_tiles = tile_end[-1]
    tile_ids = jnp.arange(n_tiles_max, dtype=jnp.int32)
    tile_valid = (tile_ids < n_tiles).astype(jnp.int32)
    tile_expert = jnp.sum(jnp.minimum(tile_ids, n_tiles - 1)[:, None] >= tile_end[None, :], axis=1,
                          dtype=jnp.int32)
    n_rows = n_tiles_max * TM_MOE
    tok = jnp.broadcast_to(jnp.arange(T, dtype=jnp.int32)[:, None], (T, 2))
    src = jnp.zeros((n_rows,), jnp.int32).at[pos.reshape(-1)].set(tok.reshape(-1))
    gate_s = jnp.zeros((n_rows,), F32).at[pos.reshape(-1)].set(gates.reshape(-1))

    ys = _run_moe(tile_expert, tile_valid, src.reshape(n_tiles_max, 1, TM_MOE),
                  gate_s[:, None], t2, w_gate[l].astype(BF16), w_up[l].astype(BF16), w_down[l].astype(BF16))

    pos3 = pos.reshape(T // TM_FIN, TM_FIN, 2).transpose(0, 2, 1).reshape(T // TM_FIN, 1, 2 * TM_FIN)
    out = _run_final(pos3, h2, ys, final_norm_g[None, :])
    return out.reshape(B, S, D)
```

```python
import functools
import math

import jax
import jax.numpy as jnp
from jax import lax
from jax.experimental import pallas as pl
from jax.experimental.pallas import tpu as pltpu

F32 = jnp.float32
BF16 = jnp.bfloat16

D_MODEL = 1024
MLA_HEADS = 8
MLA_QK_NOPE = 64
MLA_QK_ROPE = 32
MLA_V_DIM = 64
MLA_Q_RANK = 384
MLA_KV_RANK = 256
DIFF_HEADS = 4
DIFF_HALF = 64
DIFF_V_DIM = 2 * DIFF_HALF
N_GROUPS = 4
EXPERTS_PER_GROUP = 8
N_EXPERTS = N_GROUPS * EXPERTS_PER_GROUP
D_FF_EXPERT = 256
ROPE_THETA = 10000.0
EPS = 1e-6
LAM_INIT = 0.8 - 0.6 * math.exp(-0.3 * 0)

LANES = 128
N_QK_BLOCKS = MLA_HEADS + DIFF_HEADS
N_V_BLOCKS = MLA_HEADS // 2 + DIFF_HEADS
N_UNITS = MLA_HEADS // 2 + DIFF_HEADS
PROJ_COLS = MLA_Q_RANK + MLA_KV_RANK + LANES + 3 * DIFF_HEADS * DIFF_V_DIM
LOG2E = 1.4426950408889634
NEG_BIG = -1e30

TM_PROJ = 512
TQ = 256
TK = 256
TM_POST = 512
TM_MOE = 128
TM_FIN = 256
SUBLANES = 8
PAIRS_PER_GROUP = EXPERTS_PER_GROUP * (EXPERTS_PER_GROUP - 1) // 2
N_CLASSES = N_GROUPS * PAIRS_PER_GROUP
VMEM_LIMIT = 48 * 1024 * 1024


def _rms(x, g):
    return x * lax.rsqrt(jnp.mean(x * x, axis=-1, keepdims=True) + EPS) * g


def _proj_kernel(x_ref, pos_ref, freq_ref, g_attn_ref, w_in_ref, gq_ref, w_uq_ref, gkv_ref, w_ukv_ref,
                 qa_ref, ka_ref, va_ref):
    xn = _rms(x_ref[0], g_attn_ref[...]).astype(BF16)
    proj = jnp.dot(xn, w_in_ref[...], preferred_element_type=F32)

    ang = pos_ref[0] * freq_ref[...]
    cos_t, sin_t = jnp.cos(ang), jnp.sin(ang)
    lane = lax.broadcasted_iota(jnp.int32, ang.shape, 1)
    in_rope = (lane >= 64) & (lane < 96)
    cos_a = jnp.where(in_rope, cos_t, 1.0)
    sin_a_hi = jnp.where((lane >= 80) & (lane < 96), sin_t, 0.0)
    sin_a_lo = jnp.where((lane >= 64) & (lane < 80), -sin_t, 0.0)
    cos_b = jnp.where(lane < 64, cos_t, pltpu.roll(cos_t, 64, 1))
    sin_b = jnp.where(lane < 64, sin_t, pltpu.roll(sin_t, 64, 1))
    sub = lane % 64
    sin_b_hi = jnp.where(sub >= 32, sin_b, 0.0)
    sin_b_lo = jnp.where(sub < 32, -sin_b, 0.0)

    def rope_a(xb, scale):
        return (xb * (cos_a * scale) + pltpu.roll(xb, 16, 1) * (sin_a_hi * scale)
                + pltpu.roll(xb, 112, 1) * (sin_a_lo * scale))

    def rope_b(xb, scale):
        return (xb * (cos_b * scale) + pltpu.roll(xb, 32, 1) * (sin_b_hi * scale)
                + pltpu.roll(xb, 96, 1) * (sin_b_lo * scale))

    c_q = proj[:, :MLA_Q_RANK]
    c_kv = proj[:, MLA_Q_RANK:MLA_Q_RANK + MLA_KV_RANK]
    base = MLA_Q_RANK + MLA_KV_RANK
    kpe = rope_a(proj[:, base:base + LANES], 1.0)
    base += LANES

    q = jnp.dot(_rms(c_q, gq_ref[...]).astype(BF16), w_uq_ref[...], preferred_element_type=F32)
    kv = jnp.dot(_rms(c_kv, gkv_ref[...]).astype(BF16), w_ukv_ref[...], preferred_element_type=F32)

    scale_mla = (MLA_QK_NOPE + MLA_QK_ROPE) ** -0.5 * LOG2E
    scale_diff = DIFF_HALF ** -0.5 * LOG2E
    for h in range(MLA_HEADS):
        sl = slice(h * LANES, (h + 1) * LANES)
        qa_ref[0, h] = rope_a(q[:, sl], scale_mla).astype(BF16)
        ka_ref[0, h] = (kv[:, sl] + kpe).astype(BF16)
    vbase = MLA_HEADS * LANES
    for hp in range(MLA_HEADS // 2):
        va_ref[0, hp] = kv[:, vbase + hp * LANES: vbase + (hp + 1) * LANES].astype(BF16)
    width = DIFF_HEADS * DIFF_V_DIM
    for h in range(DIFF_HEADS):
        sl = slice(base + h * LANES, base + (h + 1) * LANES)
        qa_ref[0, MLA_HEADS + h] = rope_b(proj[:, sl], scale_diff).astype(BF16)
        sl = slice(base + width + h * LANES, base + width + (h + 1) * LANES)
        ka_ref[0, MLA_HEADS + h] = rope_b(proj[:, sl], 1.0).astype(BF16)
        sl = slice(base + 2 * width + h * LANES, base + 2 * width + (h + 1) * LANES)
        va_ref[0, MLA_HEADS // 2 + h] = proj[:, sl].astype(BF16)


def _run_proj(x, pos_f, freq, g_attn, w_in_r, gq, w_uq_r, gkv, w_ukv_r):
    B, S, _ = x.shape
    tm = TM_PROJ
    full = lambda shape: pl.BlockSpec(shape, lambda b, i: (0,) * len(shape))
    head_out = lambda n: pl.BlockSpec((1, n, tm, LANES), lambda b, i: (b, 0, i, 0))
    return pl.pallas_call(
        _proj_kernel,
        grid=(B, S // tm),
        in_specs=[
            pl.BlockSpec((1, tm, D_MODEL), lambda b, i: (b, i, 0)),
            pl.BlockSpec((1, tm, 1), lambda b, i: (b, i, 0)),
            full((1, LANES)),
            full((1, D_MODEL)),
            full((D_MODEL, PROJ_COLS)),
            full((1, MLA_Q_RANK)),
            full((MLA_Q_RANK, MLA_HEADS * LANES)),
            full((1, MLA_KV_RANK)),
            full((MLA_KV_RANK, MLA_HEADS * LANES + MLA_HEADS * MLA_V_DIM)),
        ],
        out_specs=[head_out(N_QK_BLOCKS), head_out(N_QK_BLOCKS), head_out(N_V_BLOCKS)],
        out_shape=[
            jax.ShapeDtypeStruct((B, N_QK_BLOCKS, S, LANES), BF16),
            jax.ShapeDtypeStruct((B, N_QK_BLOCKS, S, LANES), BF16),
            jax.ShapeDtypeStruct((B, N_V_BLOCKS, S, LANES), BF16),
        ],
        compiler_params=pltpu.CompilerParams(
            dimension_semantics=("arbitrary", "arbitrary"), vmem_limit_bytes=VMEM_LIMIT),
        name="proj",
    )(x, pos_f, freq, g_attn, w_in_r, gq, w_uq_r, gkv, w_ukv_r)


def _attn_kernel(qa_ref, ka_ref, va_ref, lam_ref, subln_ref, o_ref, m_ref, l_ref, acc_ref):
    qi = pl.program_id(1)
    m_ref[...] = jnp.full(m_ref.shape, NEG_BIG, F32)
    l_ref[...] = jnp.zeros(l_ref.shape, F32)
    acc_ref[...] = jnp.zeros(acc_ref.shape, F32)

    lane = lax.broadcasted_iota(jnp.int32, (TQ, LANES), 1)
    low_half = lane < 64

    def unit_operands(u):
        if u < MLA_HEADS // 2:
            return (qa_ref[0, 2 * u], 2 * u), (qa_ref[0, 2 * u + 1], 2 * u + 1), u
        h = u - MLA_HEADS // 2
        q = qa_ref[0, MLA_HEADS + h]
        zero = jnp.zeros_like(q)
        return ((jnp.where(low_half, q, zero), MLA_HEADS + h),
                (jnp.where(low_half, zero, q), MLA_HEADS + h), MLA_HEADS // 2 + h)

    def stream_step(slot, q, k, v, mask):
        s = lax.dot_general(q, k, (((1,), (1,)), ((), ())), preferred_element_type=F32)
        if mask is not None:
            s = jnp.where(mask, s, NEG_BIG)
        m_prev = m_ref[slot]
        m_new = jnp.maximum(m_prev, jnp.max(s, axis=1, keepdims=True))
        alpha = jnp.exp2(m_prev - m_new)
        p = jnp.exp2(s - jnp.concatenate([m_new] * (TK // LANES), axis=1))
        l_ref[slot] = alpha * l_ref[slot] + jnp.sum(p, axis=1, keepdims=True)
        acc_ref[slot] = alpha * acc_ref[slot] + jnp.dot(p.astype(BF16), v, preferred_element_type=F32)
        m_ref[slot] = m_new

    def kv_step(j, mask):
        rows = pl.ds(pl.multiple_of(j * TK, TK), TK)
        for u in range(N_UNITS):
            (q0, k0), (q1, k1), vi = unit_operands(u)
            v = va_ref[0, vi, rows, :]
            stream_step(2 * u, q0, ka_ref[0, k0, rows, :], v, mask)
            stream_step(2 * u + 1, q1, ka_ref[0, k1, rows, :], v, mask)

    def body(j, carry):
        kv_step(j, None)
        return carry

    lax.fori_loop(0, qi, body, 0)
    row = lax.broadcasted_iota(jnp.int32, (TQ, TK), 0)
    col = lax.broadcasted_iota(jnp.int32, (TQ, TK), 1)
    kv_step(qi, col <= row)

    lam_v = lam_ref[...]
    lam = (jnp.exp(jnp.sum(lam_v[0:1] * lam_v[1:2], axis=1, keepdims=True))
           - jnp.exp(jnp.sum(lam_v[2:3] * lam_v[3:4], axis=1, keepdims=True)) + LAM_INIT)
    for u in range(N_UNITS):
        o0 = acc_ref[2 * u] / l_ref[2 * u]
        o1 = acc_ref[2 * u + 1] / l_ref[2 * u + 1]
        if u < MLA_HEADS // 2:
            out = jnp.where(low_half, o0, o1)
            col0 = u * LANES
        else:
            out = _rms(o0 - lam * o1, subln_ref[...]) * (1.0 - LAM_INIT)
            col0 = MLA_HEADS * MLA_V_DIM + (u - MLA_HEADS // 2) * LANES
        o_ref[0, :, col0:col0 + LANES] = out.astype(BF16)


def _run_attn(qa, ka, va, lam_vecs, subln):
    B, _, S, _ = qa.shape
    n_slots = 2 * N_UNITS
    return pl.pallas_call(
        _attn_kernel,
        grid=(B, S // TQ),
        in_specs=[
            pl.BlockSpec((1, N_QK_BLOCKS, TQ, LANES), lambda b, i: (b, 0, i, 0)),
            pl.BlockSpec((1, N_QK_BLOCKS, S, LANES), lambda b, i: (b, 0, 0, 0)),
            pl.BlockSpec((1, N_V_BLOCKS, S, LANES), lambda b, i: (b, 0, 0, 0)),
            pl.BlockSpec((4, DIFF_HALF), lambda b, i: (0, 0)),
            pl.BlockSpec((1, DIFF_V_DIM), lambda b, i: (0, 0)),
        ],
        out_specs=pl.BlockSpec((1, TQ, D_MODEL), lambda b, i: (b, i, 0)),
        out_shape=jax.ShapeDtypeStruct((B, S, D_MODEL), BF16),
        scratch_shapes=[
            pltpu.VMEM((n_slots, TQ, LANES), F32),
            pltpu.VMEM((n_slots, TQ, LANES), F32),
            pltpu.VMEM((n_slots, TQ, LANES), F32),
        ],
        compiler_params=pltpu.CompilerParams(
            dimension_semantics=("arbitrary", "arbitrary"), vmem_limit_bytes=VMEM_LIMIT),
        name="attn",
    )(qa, ka, va, lam_vecs, subln)


def _post_kernel(o_ref, x_ref, w_o_ref, g_ffn_ref, w_r_ref, b_r_ref, tri_ref,
                 h_ref, tp_ref, route_ref, cnt_ref):
    tm = x_ref.shape[0]

    @pl.when(pl.program_id(0) == 0)
    def _():
        cnt_ref[...] = jnp.zeros(cnt_ref.shape, F32)

    h = x_ref[...] + jnp.dot(o_ref[...], w_o_ref[...], preferred_element_type=F32)
    h_ref[...] = h
    t = _rms(h, g_ffn_ref[...])
    lg = jnp.dot(t.astype(BF16), w_r_ref[...], preferred_element_type=F32) + b_r_ref[...]

    lane = lax.broadcasted_iota(jnp.int32, lg.shape, 1)
    lane_f = lane.astype(F32)
    first_lane = lambda hit: jnp.min(jnp.where(hit, lane_f, float(LANES)), axis=1, keepdims=True)

    is_group = lane < N_GROUPS
    gl = jnp.where(is_group, lg, NEG_BIG)
    gmax = jnp.max(gl, axis=1, keepdims=True)
    gidx = first_lane(gl == gmax)

    lo = N_GROUPS + EXPERTS_PER_GROUP * gidx
    el = jnp.where((lane_f >= lo) & (lane_f < lo + EXPERTS_PER_GROUP), lg, NEG_BIG)
    m1 = jnp.max(el, axis=1, keepdims=True)
    i1 = first_lane(el == m1)
    el2 = jnp.where(lane_f == i1, NEG_BIG, el)
    m2 = jnp.max(el2, axis=1, keepdims=True)
    i2 = first_lane(el2 == m2)

    a = jnp.minimum(i1, i2) - lo
    b = jnp.maximum(i1, i2) - lo
    pair = a * (2 * EXPERTS_PER_GROUP - 1 - a) * 0.5 + (b - a - 1.0)
    cls = gidx * PAIRS_PER_GROUP + pair

    hit = lane_f == cls
    onehot = jnp.where(hit, 1.0, 0.0)
    before = jnp.dot(tri_ref[...], onehot.astype(BF16), preferred_element_type=F32) + cnt_ref[...]
    rank = jnp.sum(jnp.where(hit, before, 0.0), axis=1, keepdims=True)
    cnt_ref[...] += jnp.sum(onehot, axis=0, keepdims=True)

    out = jnp.zeros(lg.shape, F32)
    for k, val in enumerate((cls, rank)):
        out = jnp.where(lane == k, val, out)
    route_ref[...] = out

    for c in range(SUBLANES):
        tp_ref[pl.ds(c, tm, stride=SUBLANES), :] = t[:, c * LANES:(c + 1) * LANES]


def _run_post(o2, x2, w_o, g_ffn, w_r, b_r, tri):
    T = x2.shape[0]
    tm = TM_POST
    full = lambda shape: pl.BlockSpec(shape, lambda i: (0,) * len(shape))
    rows = lambda w: pl.BlockSpec((tm, w), lambda i: (i, 0))
    return pl.pallas_call(
        _post_kernel,
        grid=(T // tm,),
        in_specs=[rows(D_MODEL), rows(D_MODEL), full((D_MODEL, D_MODEL)), full((1, D_MODEL)),
                  full((D_MODEL, LANES)), full((1, LANES)), full((tm, tm))],
        out_specs=[rows(D_MODEL), pl.BlockSpec((tm * SUBLANES, LANES), lambda i: (i, 0)), rows(LANES),
                   full((1, LANES))],
        out_shape=[
            jax.ShapeDtypeStruct((T, D_MODEL), F32),
            jax.ShapeDtypeStruct((T * SUBLANES, LANES), F32),
            jax.ShapeDtypeStruct((T, LANES), F32),
            jax.ShapeDtypeStruct((1, LANES), F32),
        ],
        compiler_params=pltpu.CompilerParams(
            dimension_semantics=("arbitrary",), vmem_limit_bytes=VMEM_LIMIT),
        name="post",
    )(o2, x2, w_o, g_ffn, w_r, b_r, tri)


TILE_ROWS = TM_MOE * SUBLANES


def _moe_kernel(key_ref, off_ref, ta_ref, tb_ref, tv_ref, tp_hbm, w_r_ref, b_r_ref,
                wga_ref, wua_ref, wda_ref, wgb_ref, wub_ref, wdb_ref,
                y_ref, pos_ref, src_ref, xbuf, sem):
    i = pl.program_id(0)
    n_tok = key_ref.shape[0]
    tok_bits = n_tok.bit_length() - 1

    def start_gather(tile, slot):
        for r in range(TM_MOE):
            tok = src_ref[tile * TM_MOE + r]
            pltpu.make_async_copy(
                tp_hbm.at[pl.ds(pl.multiple_of(tok * SUBLANES, SUBLANES), SUBLANES), :],
                xbuf.at[pl.ds(pl.multiple_of(slot * TILE_ROWS + r * SUBLANES, SUBLANES), SUBLANES), :],
                sem.at[slot]).start()

    def wait_gather(slot):
        pltpu.make_async_copy(
            tp_hbm.at[pl.ds(0, TILE_ROWS), :],
            xbuf.at[pl.ds(pl.multiple_of(slot * TILE_ROWS, SUBLANES), TILE_ROWS), :],
            sem.at[slot]).wait()

    @pl.when(i == 0)
    def _():
        def clear(r, c):
            src_ref[r] = 0
            return c

        def invert(t, c):
            key = key_ref[t]
            p = off_ref[key >> tok_bits] + (key & (n_tok - 1))
            src_ref[p] = t
            pos_ref[t] = p
            return c

        lax.fori_loop(0, src_ref.shape[0], clear, 0, unroll=8)
        lax.fori_loop(0, n_tok, invert, 0, unroll=8)
        start_gather(0, 0)

    slot = i % 2

    @pl.when(tv_ref[i] > 0)
    def _():
        start_gather(i + 1, 1 - slot)
        wait_gather(slot)
        base = slot * TILE_ROWS
        x = jnp.concatenate([xbuf[pl.ds(base + c, TM_MOE, stride=SUBLANES), :] for c in range(SUBLANES)],
                            axis=1).astype(BF16)

        lg = jnp.dot(x, w_r_ref[...], preferred_element_type=F32) + b_r_ref[...]
        lane = lax.broadcasted_iota(jnp.int32, lg.shape, 1)
        pick = lambda idx: jnp.sum(jnp.where(lane == idx, lg, 0.0), axis=1, keepdims=True)
        ea, eb = ta_ref[i], tb_ref[i]
        g_logit = pick(ea // EXPERTS_PER_GROUP)
        p_group = 1.0 / jnp.sum(jnp.where(lane < N_GROUPS, jnp.exp(lg - g_logit), 0.0), axis=1, keepdims=True)
        l_a, l_b = pick(N_GROUPS + ea), pick(N_GROUPS + eb)
        gate_a = p_group / (1.0 + jnp.exp(l_b - l_a))
        gate_b = p_group / (1.0 + jnp.exp(l_a - l_b))

        def hidden(wg_ref, wu_ref, gate):
            a = jnp.dot(x, wg_ref[0], preferred_element_type=F32)
            u = jnp.dot(x, wu_ref[0], preferred_element_type=F32)
            return ((a * jax.nn.sigmoid(a)) * u * gate).astype(BF16)

        y = (jnp.dot(hidden(wga_ref, wua_ref, gate_a), wda_ref[0], preferred_element_type=F32)
             + jnp.dot(hidden(wgb_ref, wub_ref, gate_b), wdb_ref[0], preferred_element_type=F32))
        for c in range(SUBLANES):
            y_ref[pl.ds(c, TM_MOE, stride=SUBLANES), :] = y[:, c * LANES:(c + 1) * LANES]

    @pl.when(tv_ref[i] == 0)
    def _():
        y_ref[...] = jnp.zeros(y_ref.shape, F32)

        @pl.when(tv_ref[jnp.maximum(i - 1, 0)] > 0)
        def _():
            wait_gather(slot)


def _run_moe(key, row_off, tile_a, tile_b, tile_valid, tp8, w_r, b_r, wg, wu, wd):
    n_tiles = tile_valid.shape[0]
    n_tok = key.shape[0]
    assert n_tok & (n_tok - 1) == 0, "token count must be a power of two (class/rank key packing)"
    w_in_spec = lambda sel: pl.BlockSpec((1, D_MODEL, D_FF_EXPERT), lambda i, k, o, ta, tb, tv: (sel(ta, tb)[i], 0, 0))
    w_out_spec = lambda sel: pl.BlockSpec((1, D_FF_EXPERT, D_MODEL), lambda i, k, o, ta, tb, tv: (sel(ta, tb)[i], 0, 0))
    first = lambda ta, tb: ta
    second = lambda ta, tb: tb
    grid_spec = pltpu.PrefetchScalarGridSpec(
        num_scalar_prefetch=5,
        grid=(n_tiles,),
        in_specs=[
            pl.BlockSpec(memory_space=pl.ANY),
            pl.BlockSpec((D_MODEL, LANES), lambda i, k, o, ta, tb, tv: (0, 0)),
            pl.BlockSpec((1, LANES), lambda i, k, o, ta, tb, tv: (0, 0)),
            w_in_spec(first), w_in_spec(first), w_out_spec(first),
            w_in_spec(second), w_in_spec(second), w_out_spec(second),
        ],
        out_specs=[
            pl.BlockSpec((TILE_ROWS, LANES), lambda i, k, o, ta, tb, tv: (i, 0)),
            pl.BlockSpec(memory_space=pltpu.SMEM),
        ],
        scratch_shapes=[
            pltpu.SMEM((n_tiles * TM_MOE,), jnp.int32),
            pltpu.VMEM((2 * TILE_ROWS, LANES), F32),
            pltpu.SemaphoreType.DMA((2,)),
        ],
    )
    return pl.pallas_call(
        _moe_kernel,
        grid_spec=grid_spec,
        out_shape=[
            jax.ShapeDtypeStruct((n_tiles * TILE_ROWS, LANES), F32),
            jax.ShapeDtypeStruct((n_tok,), jnp.int32),
        ],
        compiler_params=pltpu.CompilerParams(
            dimension_semantics=("arbitrary",), vmem_limit_bytes=VMEM_LIMIT),
        name="moe",
    )(key, row_off, tile_a, tile_b, tile_valid, tp8, w_r, b_r, wg, wu, wd, wg, wu, wd)


FIN_ROWS = TM_FIN * SUBLANES


def _final_kernel(pos_ref, h_ref, y_hbm, g_ref, out_ref, ybuf, sem):
    i = pl.program_id(0)
    slot = i % 2

    def start_gather(tile, slot):
        def body(r, c):
            p = pos_ref[tile * TM_FIN + r]
            pltpu.make_async_copy(
                y_hbm.at[pl.ds(pl.multiple_of(p * SUBLANES, SUBLANES), SUBLANES), :],
                ybuf.at[pl.ds(pl.multiple_of(slot * FIN_ROWS + r * SUBLANES, SUBLANES), SUBLANES), :],
                sem.at[slot]).start()
            return c
        lax.fori_loop(0, TM_FIN, body, 0, unroll=8)

    @pl.when(i == 0)
    def _():
        start_gather(0, 0)

    @pl.when(i + 1 < pl.num_programs(0))
    def _():
        start_gather(i + 1, 1 - slot)

    pltpu.make_async_copy(
        y_hbm.at[pl.ds(0, FIN_ROWS), :],
        ybuf.at[pl.ds(pl.multiple_of(slot * FIN_ROWS, SUBLANES), FIN_ROWS), :],
        sem.at[slot]).wait()
    base = slot * FIN_ROWS
    y = jnp.concatenate([ybuf[pl.ds(base + c, TM_FIN, stride=SUBLANES), :] for c in range(SUBLANES)], axis=1)
    out_ref[...] = _rms(h_ref[...] + y, g_ref[...])


def _run_final(pos, h2, ys8, g_final):
    T = h2.shape[0]
    grid_spec = pltpu.PrefetchScalarGridSpec(
        num_scalar_prefetch=1,
        grid=(T // TM_FIN,),
        in_specs=[
            pl.BlockSpec((TM_FIN, D_MODEL), lambda i, p: (i, 0)),
            pl.BlockSpec(memory_space=pl.ANY),
            pl.BlockSpec((1, D_MODEL), lambda i, p: (0, 0)),
        ],
        out_specs=pl.BlockSpec((TM_FIN, D_MODEL), lambda i, p: (i, 0)),
        scratch_shapes=[pltpu.VMEM((2 * FIN_ROWS, LANES), F32), pltpu.SemaphoreType.DMA((2,))],
    )
    return pl.pallas_call(
        _final_kernel,
        grid_spec=grid_spec,
        out_shape=jax.ShapeDtypeStruct((T, D_MODEL), F32),
        compiler_params=pltpu.CompilerParams(
            dimension_semantics=("arbitrary",), vmem_limit_bytes=VMEM_LIMIT),
        name="final",
    )(pos, h2, ys8, g_final)


def _prep_attention_weights(w_in, w_uq, w_ukv):
    d = w_in.shape[0]
    c0 = MLA_Q_RANK + MLA_KV_RANK
    kpe = w_in[:, c0:c0 + MLA_QK_ROPE]
    kpe_blk = jnp.concatenate(
        [jnp.zeros((d, MLA_QK_NOPE), F32), kpe, jnp.zeros((d, LANES - MLA_QK_NOPE - MLA_QK_ROPE), F32)], axis=1)
    w_in_r = jnp.concatenate([w_in[:, :c0], kpe_blk, w_in[:, c0 + MLA_QK_ROPE:]], axis=1).astype(BF16)

    dqk = MLA_QK_NOPE + MLA_QK_ROPE
    uq = w_uq.reshape(MLA_Q_RANK, MLA_HEADS, dqk)
    uq = jnp.pad(uq, ((0, 0), (0, 0), (0, LANES - dqk))).reshape(MLA_Q_RANK, MLA_HEADS * LANES).astype(BF16)

    ukv = w_ukv.reshape(MLA_KV_RANK, MLA_HEADS, MLA_QK_NOPE + MLA_V_DIM)
    uk = jnp.pad(ukv[:, :, :MLA_QK_NOPE], ((0, 0), (0, 0), (0, LANES - MLA_QK_NOPE)))
    uk = uk.reshape(MLA_KV_RANK, MLA_HEADS * LANES)
    uv = ukv[:, :, MLA_QK_NOPE:].reshape(MLA_KV_RANK, MLA_HEADS * MLA_V_DIM)
    w_ukv_r = jnp.concatenate([uk, uv], axis=1).astype(BF16)
    return w_in_r, uq, w_ukv_r


def _rope_freq_lanes():
    fb = ROPE_THETA ** (-jnp.arange(DIFF_HALF // 2, dtype=F32) / (DIFF_HALF // 2))
    fa = ROPE_THETA ** (-jnp.arange(MLA_QK_ROPE // 2, dtype=F32) / (MLA_QK_ROPE // 2))
    return jnp.concatenate([fb, fb, fa, fa, jnp.zeros((LANES - 96,), F32)])[None, :]


def kernel(x, positions, attn_norm_g, w_in, q_norm_g, w_uq, kv_norm_g, w_ukv, lambda_q1, lambda_k1, lambda_q2, lambda_k2, subln_g, w_o, ffn_norm_g, w_router_group, b_router_group, w_router_expert, b_router_expert, w_gate, w_up, w_down, final_norm_g):
    B, S, D = x.shape
    T = B * S
    l = 0

    w_in_r, w_uq_r, w_ukv_r = _prep_attention_weights(w_in[l], w_uq[l], w_ukv[l])
    pos_f = positions.astype(F32)[:, :, None]
    qa, ka, va = _run_proj(x, pos_f, _rope_freq_lanes(), attn_norm_g[l][None, :], w_in_r,
                           q_norm_g[l][None, :], w_uq_r, kv_norm_g[l][None, :], w_ukv_r)

    lam_vecs = jnp.stack([lambda_q1[l], lambda_k1[l], lambda_q2[l], lambda_k2[l]]).astype(F32)
    o = _run_attn(qa, ka, va, lam_vecs, subln_g[l][None, :])

    w_r = jnp.concatenate(
        [w_router_group[l], w_router_expert[l].reshape(D, N_EXPERTS),
         jnp.zeros((D, LANES - N_GROUPS - N_EXPERTS), F32)], axis=1).astype(BF16)
    b_r = jnp.concatenate(
        [b_router_group[l], b_router_expert[l].reshape(N_EXPERTS),
         jnp.zeros((LANES - N_GROUPS - N_EXPERTS,), F32)])[None, :].astype(F32)
    ii = jnp.arange(TM_POST)
    tri = (ii[None, :] < ii[:, None]).astype(BF16)
    x2 = x.reshape(T, D)
    h2, tp8, route, counts = _run_post(o.reshape(T, D), x2, w_o[l].astype(BF16), ffn_norm_g[l][None, :],
                                       w_r, b_r, tri)

    n_tiles_max = T // TM_MOE + N_CLASSES
    key = route[:, 0].astype(jnp.int32) * T + route[:, 1].astype(jnp.int32)
    cnt = counts[0, :N_CLASSES].astype(jnp.int32)
    tiles_c = (cnt + TM_MOE - 1) // TM_MOE
    tile_end = jnp.cumsum(tiles_c)
    row_off = jnp.zeros((LANES,), jnp.int32).at[:N_CLASSES].set((tile_end - tiles_c) * TM_MOE)
    n_tiles = tile_end[-1]
    tile_ids = jnp.arange(n_tiles_max, dtype=jnp.int32)
    tile_valid = (tile_ids < n_tiles).astype(jnp.int32)
    tile_cls = jnp.sum(jnp.minimum(tile_ids, n_tiles - 1)[:, None] >= tile_end[None, :], axis=1, dtype=jnp.int32)
    pair_a, pair_b = [], []
    for g in range(N_GROUPS):
        for a in range(EXPERTS_PER_GROUP):
            for b in range(a + 1, EXPERTS_PER_GROUP):
                pair_a.append(g * EXPERTS_PER_GROUP + a)
                pair_b.append(g * EXPERTS_PER_GROUP + b)
    cls_onehot = (tile_cls[:, None] == jnp.arange(N_CLASSES, dtype=jnp.int32)[None, :]).astype(jnp.int32)
    tile_a = cls_onehot @ jnp.array(pair_a, jnp.int32)
    tile_b = cls_onehot @ jnp.array(pair_b, jnp.int32)

    ys8, pos = _run_moe(key, row_off, tile_a, tile_b, tile_valid, tp8, w_r, b_r,
                        w_gate[l].astype(BF16), w_up[l].astype(BF16), w_down[l].astype(BF16))
    out = _run_final(pos, h2, ys8, final_norm_g[None, :])
    return out.reshape(B, S, D)
```

```python
import functools
import math

import jax
import jax.numpy as jnp
from jax import lax
from jax.experimental import pallas as pl
from jax.experimental.pallas import tpu as pltpu

F32 = jnp.float32
BF16 = jnp.bfloat16

D_MODEL = 1024
MLA_HEADS = 8
MLA_QK_NOPE = 64
MLA_QK_ROPE = 32
MLA_V_DIM = 64
MLA_Q_RANK = 384
MLA_KV_RANK = 256
DIFF_HEADS = 4
DIFF_HALF = 64
DIFF_V_DIM = 2 * DIFF_HALF
N_GROUPS = 4
EXPERTS_PER_GROUP = 8
N_EXPERTS = N_GROUPS * EXPERTS_PER_GROUP
D_FF_EXPERT = 256
ROPE_THETA = 10000.0
EPS = 1e-6
LAM_INIT = 0.8 - 0.6 * math.exp(-0.3 * 0)

LANES = 128
N_QK_BLOCKS = MLA_HEADS + DIFF_HEADS
N_V_BLOCKS = MLA_HEADS // 2 + DIFF_HEADS
N_UNITS = MLA_HEADS // 2 + DIFF_HEADS
PROJ_COLS = MLA_Q_RANK + MLA_KV_RANK + LANES + 3 * DIFF_HEADS * DIFF_V_DIM
LOG2E = 1.4426950408889634
NEG_BIG = -1e30

TM_PROJ = 512
TQ = 256
TK = 256
TM_POST = 512
TM_MOE = 128
TM_FIN = 256
SUBLANES = 8
PAIRS_PER_GROUP = EXPERTS_PER_GROUP * (EXPERTS_PER_GROUP - 1) // 2
N_CLASSES = N_GROUPS * PAIRS_PER_GROUP
VMEM_LIMIT = 48 * 1024 * 1024


def _rms(x, g):
    return x * lax.rsqrt(jnp.mean(x * x, axis=-1, keepdims=True) + EPS) * g


def _proj_kernel(x_ref, pos_ref, freq_ref, g_attn_ref, w_in_ref, gq_ref, w_uq_ref, gkv_ref, w_ukv_ref,
                 qa_ref, ka_ref, va_ref):
    xn = _rms(x_ref[0], g_attn_ref[...]).astype(BF16)
    proj = jnp.dot(xn, w_in_ref[...], preferred_element_type=F32)

    ang = pos_ref[0] * freq_ref[...]
    cos_t, sin_t = jnp.cos(ang), jnp.sin(ang)
    lane = lax.broadcasted_iota(jnp.int32, ang.shape, 1)
    in_rope = (lane >= 64) & (lane < 96)
    cos_a = jnp.where(in_rope, cos_t, 1.0)
    sin_a_hi = jnp.where((lane >= 80) & (lane < 96), sin_t, 0.0)
    sin_a_lo = jnp.where((lane >= 64) & (lane < 80), -sin_t, 0.0)
    cos_b = jnp.where(lane < 64, cos_t, pltpu.roll(cos_t, 64, 1))
    sin_b = jnp.where(lane < 64, sin_t, pltpu.roll(sin_t, 64, 1))
    sub = lane % 64
    sin_b_hi = jnp.where(sub >= 32, sin_b, 0.0)
    sin_b_lo = jnp.where(sub < 32, -sin_b, 0.0)

    def rope_a(xb, scale):
        return (xb * (cos_a * scale) + pltpu.roll(xb, 16, 1) * (sin_a_hi * scale)
                + pltpu.roll(xb, 112, 1) * (sin_a_lo * scale))

    def rope_b(xb, scale):
        return (xb * (cos_b * scale) + pltpu.roll(xb, 32, 1) * (sin_b_hi * scale)
                + pltpu.roll(xb, 96, 1) * (sin_b_lo * scale))

    c_q = proj[:, :MLA_Q_RANK]
    c_kv = proj[:, MLA_Q_RANK:MLA_Q_RANK + MLA_KV_RANK]
    base = MLA_Q_RANK + MLA_KV_RANK
    kpe = rope_a(proj[:, base:base + LANES], 1.0)
    base += LANES

    q = jnp.dot(_rms(c_q, gq_ref[...]).astype(BF16), w_uq_ref[...], preferred_element_type=F32)
    kv = jnp.dot(_rms(c_kv, gkv_ref[...]).astype(BF16), w_ukv_ref[...], preferred_element_type=F32)

    scale_mla = (MLA_QK_NOPE + MLA_QK_ROPE) ** -0.5 * LOG2E
    scale_diff = DIFF_HALF ** -0.5 * LOG2E
    for h in range(MLA_HEADS):
        sl = slice(h * LANES, (h + 1) * LANES)
        qa_ref[0, h] = rope_a(q[:, sl], scale_mla).astype(BF16)
        ka_ref[0, h] = (kv[:, sl] + kpe).astype(BF16)
    vbase = MLA_HEADS * LANES
    for hp in range(MLA_HEADS // 2):
        va_ref[0, hp] = kv[:, vbase + hp * LANES: vbase + (hp + 1) * LANES].astype(BF16)
    width = DIFF_HEADS * DIFF_V_DIM
    for h in range(DIFF_HEADS):
        sl = slice(base + h * LANES, base + (h + 1) * LANES)
        qa_ref[0, MLA_HEADS + h] = rope_b(proj[:, sl], scale_diff).astype(BF16)
        sl = slice(base + width + h * LANES, base + width + (h + 1) * LANES)
        ka_ref[0, MLA_HEADS + h] = rope_b(proj[:, sl], 1.0).astype(BF16)
        sl = slice(base + 2 * width + h * LANES, base + 2 * width + (h + 1) * LANES)
        va_ref[0, MLA_HEADS // 2 + h] = proj[:, sl].astype(BF16)


def _run_proj(x, pos_f, freq, g_attn, w_in_r, gq, w_uq_r, gkv, w_ukv_r):
    B, S, _ = x.shape
    tm = TM_PROJ
    full = lambda shape: pl.BlockSpec(shape, lambda b, i: (0,) * len(shape))
    head_out = lambda n: pl.BlockSpec((1, n, tm, LANES), lambda b, i: (b, 0, i, 0))
    return pl.pallas_call(
        _proj_kernel,
        grid=(B, S // tm),
        in_specs=[
            pl.BlockSpec((1, tm, D_MODEL), lambda b, i: (b, i, 0)),
            pl.BlockSpec((1, tm, 1), lambda b, i: (b, i, 0)),
            full((1, LANES)),
            full((1, D_MODEL)),
            full((D_MODEL, PROJ_COLS)),
            full((1, MLA_Q_RANK)),
            full((MLA_Q_RANK, MLA_HEADS * LANES)),
            full((1, MLA_KV_RANK)),
            full((MLA_KV_RANK, MLA_HEADS * LANES + MLA_HEADS * MLA_V_DIM)),
        ],
        out_specs=[head_out(N_QK_BLOCKS), head_out(N_QK_BLOCKS), head_out(N_V_BLOCKS)],
        out_shape=[
            jax.ShapeDtypeStruct((B, N_QK_BLOCKS, S, LANES), BF16),
            jax.ShapeDtypeStruct((B, N_QK_BLOCKS, S, LANES), BF16),
            jax.ShapeDtypeStruct((B, N_V_BLOCKS, S, LANES), BF16),
        ],
        compiler_params=pltpu.CompilerParams(
            dimension_semantics=("arbitrary", "arbitrary"), vmem_limit_bytes=VMEM_LIMIT),
        name="proj",
    )(x, pos_f, freq, g_attn, w_in_r, gq, w_uq_r, gkv, w_ukv_r)


def _attn_kernel(qa_ref, ka_ref, va_ref, lam_ref, subln_ref, o_ref, m_ref, l_ref, acc_ref):
    qi = pl.program_id(1)
    m_ref[...] = jnp.full(m_ref.shape, NEG_BIG, F32)
    l_ref[...] = jnp.zeros(l_ref.shape, F32)
    acc_ref[...] = jnp.zeros(acc_ref.shape, F32)

    lane = lax.broadcasted_iota(jnp.int32, (TQ, LANES), 1)
    low_half = lane < 64

    def unit_operands(u):
        if u < MLA_HEADS // 2:
            return (qa_ref[0, 2 * u], 2 * u), (qa_ref[0, 2 * u + 1], 2 * u + 1), u
        h = u - MLA_HEADS // 2
        q = qa_ref[0, MLA_HEADS + h]
        zero = jnp.zeros_like(q)
        return ((jnp.where(low_half, q, zero), MLA_HEADS + h),
                (jnp.where(low_half, zero, q), MLA_HEADS + h), MLA_HEADS // 2 + h)

    def stream_step(slot, q, k, v, mask):
        s = lax.dot_general(q, k, (((1,), (1,)), ((), ())), preferred_element_type=F32)
        if mask is not None:
            s = jnp.where(mask, s, NEG_BIG)
        m_prev = m_ref[slot]
        m_new = jnp.maximum(m_prev, jnp.max(s, axis=1, keepdims=True))
        alpha = jnp.exp2(m_prev - m_new)
        p = jnp.exp2(s - jnp.concatenate([m_new] * (TK // LANES), axis=1))
        l_ref[slot] = alpha * l_ref[slot] + jnp.sum(p, axis=1, keepdims=True)
        acc_ref[slot] = alpha * acc_ref[slot] + jnp.dot(p.astype(BF16), v, preferred_element_type=F32)
        m_ref[slot] = m_new

    def kv_step(j, mask):
        rows = pl.ds(pl.multiple_of(j * TK, TK), TK)
        for u in range(N_UNITS):
            (q0, k0), (q1, k1), vi = unit_operands(u)
            v = va_ref[0, vi, rows, :]
            stream_step(2 * u, q0, ka_ref[0, k0, rows, :], v, mask)
            stream_step(2 * u + 1, q1, ka_ref[0, k1, rows, :], v, mask)

    def body(j, carry):
        kv_step(j, None)
        return carry

    lax.fori_loop(0, qi, body, 0)
    row = lax.broadcasted_iota(jnp.int32, (TQ, TK), 0)
    col = lax.broadcasted_iota(jnp.int32, (TQ, TK), 1)
    kv_step(qi, col <= row)

    lam_v = lam_ref[...]
    lam = (jnp.exp(jnp.sum(lam_v[0:1] * lam_v[1:2], axis=1, keepdims=True))
           - jnp.exp(jnp.sum(lam_v[2:3] * lam_v[3:4], axis=1, keepdims=True)) + LAM_INIT)
    for u in range(N_UNITS):
        o0 = acc_ref[2 * u] / l_ref[2 * u]
        o1 = acc_ref[2 * u + 1] / l_ref[2 * u + 1]
        if u < MLA_HEADS // 2:
            out = jnp.where(low_half, o0, o1)
            col0 = u * LANES
        else:
            out = _rms(o0 - lam * o1, subln_ref[...]) * (1.0 - LAM_INIT)
            col0 = MLA_HEADS * MLA_V_DIM + (u - MLA_HEADS // 2) * LANES
        o_ref[0, :, col0:col0 + LANES] = out.astype(BF16)


def _run_attn(qa, ka, va, lam_vecs, subln):
    B, _, S, _ = qa.shape
    n_slots = 2 * N_UNITS
    return pl.pallas_call(
        _attn_kernel,
        grid=(B, S // TQ),
        in_specs=[
            pl.BlockSpec((1, N_QK_BLOCKS, TQ, LANES), lambda b, i: (b, 0, i, 0)),
            pl.BlockSpec((1, N_QK_BLOCKS, S, LANES), lambda b, i: (b, 0, 0, 0)),
            pl.BlockSpec((1, N_V_BLOCKS, S, LANES), lambda b, i: (b, 0, 0, 0)),
            pl.BlockSpec((4, DIFF_HALF), lambda b, i: (0, 0)),
            pl.BlockSpec((1, DIFF_V_DIM), lambda b, i: (0, 0)),
        ],
        out_specs=pl.BlockSpec((1, TQ, D_MODEL), lambda b, i: (b, i, 0)),
        out_shape=jax.ShapeDtypeStruct((B, S, D_MODEL), BF16),
        scratch_shapes=[
            pltpu.VMEM((n_slots, TQ, LANES), F32),
            pltpu.VMEM((n_slots, TQ, LANES), F32),
            pltpu.VMEM((n_slots, TQ, LANES), F32),
        ],
        compiler_params=pltpu.CompilerParams(
            dimension_semantics=("arbitrary", "arbitrary"), vmem_limit_bytes=VMEM_LIMIT),
        name="attn",
    )(qa, ka, va, lam_vecs, subln)


def _post_kernel(o_ref, x_ref, w_o_ref, g_ffn_ref, w_r_ref, b_r_ref, tri_ref,
                 h_ref, tp_ref, route_ref, cnt_ref):
    tm = x_ref.shape[0]

    @pl.when(pl.program_id(0) == 0)
    def _():
        cnt_ref[...] = jnp.zeros(cnt_ref.shape, F32)

    h = x_ref[...] + jnp.dot(o_ref[...], w_o_ref[...], preferred_element_type=F32)
    h_ref[...] = h
    t = _rms(h, g_ffn_ref[...])
    lg = jnp.dot(t.astype(BF16), w_r_ref[...], preferred_element_type=F32) + b_r_ref[...]

    lane = lax.broadcasted_iota(jnp.int32, lg.shape, 1)
    lane_f = lane.astype(F32)
    first_lane = lambda hit: jnp.min(jnp.where(hit, lane_f, float(LANES)), axis=1, keepdims=True)

    is_group = lane < N_GROUPS
    gl = jnp.where(is_group, lg, NEG_BIG)
    gmax = jnp.max(gl, axis=1, keepdims=True)
    gidx = first_lane(gl == gmax)

    lo = N_GROUPS + EXPERTS_PER_GROUP * gidx
    el = jnp.where((lane_f >= lo) & (lane_f < lo + EXPERTS_PER_GROUP), lg, NEG_BIG)
    m1 = jnp.max(el, axis=1, keepdims=True)
    i1 = first_lane(el == m1)
    el2 = jnp.where(lane_f == i1, NEG_BIG, el)
    m2 = jnp.max(el2, axis=1, keepdims=True)
    i2 = first_lane(el2 == m2)

    a = jnp.minimum(i1, i2) - lo
    b = jnp.maximum(i1, i2) - lo
    pair = a * (2 * EXPERTS_PER_GROUP - 1 - a) * 0.5 + (b - a - 1.0)
    cls = gidx * PAIRS_PER_GROUP + pair

    hit = lane_f == cls
    onehot = jnp.where(hit, 1.0, 0.0)
    before = jnp.dot(tri_ref[...], onehot.astype(BF16), preferred_element_type=F32) + cnt_ref[...]
    rank = jnp.sum(jnp.where(hit, before, 0.0), axis=1, keepdims=True)
    cnt_ref[...] += jnp.sum(onehot, axis=0, keepdims=True)

    out = jnp.zeros(lg.shape, F32)
    for k, val in enumerate((cls, rank)):
        out = jnp.where(lane == k, val, out)
    route_ref[...] = out

    for c in range(SUBLANES):
        tp_ref[pl.ds(c, tm, stride=SUBLANES), :] = t[:, c * LANES:(c + 1) * LANES]


def _run_post(o2, x2, w_o, g_ffn, w_r, b_r, tri):
    T = x2.shape[0]
    tm = TM_POST
    full = lambda shape: pl.BlockSpec(shape, lambda i: (0,) * len(shape))
    rows = lambda w: pl.BlockSpec((tm, w), lambda i: (i, 0))
    return pl.pallas_call(
        _post_kernel,
        grid=(T // tm,),
        in_specs=[rows(D_MODEL), rows(D_MODEL), full((D_MODEL, D_MODEL)), full((1, D_MODEL)),
                  full((D_MODEL, LANES)), full((1, LANES)), full((tm, tm))],
        out_specs=[rows(D_MODEL), pl.BlockSpec((tm * SUBLANES, LANES), lambda i: (i, 0)), rows(LANES),
                   full((1, LANES))],
        out_shape=[
            jax.ShapeDtypeStruct((T, D_MODEL), F32),
            jax.ShapeDtypeStruct((T * SUBLANES, LANES), F32),
            jax.ShapeDtypeStruct((T, LANES), F32),
            jax.ShapeDtypeStruct((1, LANES), F32),
        ],
        compiler_params=pltpu.CompilerParams(
            dimension_semantics=("arbitrary",), vmem_limit_bytes=VMEM_LIMIT),
        name="post",
    )(o2, x2, w_o, g_ffn, w_r, b_r, tri)


TILE_ROWS = TM_MOE * SUBLANES


GATHER_AHEAD = 2
GATHER_SLOTS = GATHER_AHEAD + 1


def _sorted_row(key_ref, off_ref, t):
    n_tok = key_ref.shape[0]
    key = key_ref[t]
    return off_ref[key >> (n_tok.bit_length() - 1)] + (key & (n_tok - 1))


def _moe_kernel(key_ref, off_ref, ta_ref, tb_ref, nv_ref, tp_hbm, w_r_ref, b_r_ref, wg_ref, wu_ref, wd_ref,
                y_ref, src_ref, xbuf, sem):
    i = pl.program_id(0)
    n_tok = key_ref.shape[0]

    def start_gather(tile):
        slot = tile % GATHER_SLOTS
        for r in range(TM_MOE):
            tok = src_ref[tile * TM_MOE + r]
            pltpu.make_async_copy(
                tp_hbm.at[pl.ds(pl.multiple_of(tok * SUBLANES, SUBLANES), SUBLANES), :],
                xbuf.at[pl.ds(pl.multiple_of(slot * TILE_ROWS + r * SUBLANES, SUBLANES), SUBLANES), :],
                sem.at[slot]).start(priority=r % 2)

    def wait_gather(tile):
        slot = tile % GATHER_SLOTS
        pltpu.make_async_copy(
            tp_hbm.at[pl.ds(0, TILE_ROWS), :],
            xbuf.at[pl.ds(pl.multiple_of(slot * TILE_ROWS, SUBLANES), TILE_ROWS), :],
            sem.at[slot]).wait()

    @pl.when(i == 0)
    def _():
        def fill_pad(tile, c):
            def body(r, c2):
                src_ref[tile * TM_MOE + r] = (tile * TM_MOE + r) & (n_tok - 1)
                return c2
            return lax.fori_loop(nv_ref[tile], TM_MOE, body, c)

        def invert(t, c):
            src_ref[_sorted_row(key_ref, off_ref, t)] = t
            return c

        n_used = off_ref[N_CLASSES] // TM_MOE
        lax.fori_loop(0, n_used + GATHER_AHEAD, fill_pad, 0)
        lax.fori_loop(0, n_tok, invert, 0, unroll=16)
        for tile in range(GATHER_AHEAD):
            start_gather(tile)

    @pl.when(nv_ref[i] > 0)
    def _():
        wait_gather(i)
        base = (i % GATHER_SLOTS) * TILE_ROWS
        x = jnp.concatenate([xbuf[pl.ds(base + c, TM_MOE, stride=SUBLANES), :] for c in range(SUBLANES)],
                            axis=1).astype(BF16)

        lg = jnp.dot(x, w_r_ref[...], preferred_element_type=F32) + b_r_ref[...]
        lane = lax.broadcasted_iota(jnp.int32, lg.shape, 1)
        pick = lambda idx: jnp.sum(jnp.where(lane == idx, lg, 0.0), axis=1, keepdims=True)
        ea, eb = ta_ref[i], tb_ref[i]
        g_logit = pick(ea // EXPERTS_PER_GROUP)
        p_group = 1.0 / jnp.sum(jnp.where(lane < N_GROUPS, jnp.exp(lg - g_logit), 0.0), axis=1, keepdims=True)
        l_a, l_b = pick(N_GROUPS + ea), pick(N_GROUPS + eb)
        gate_a = p_group / (1.0 + jnp.exp(l_b - l_a))
        gate_b = p_group / (1.0 + jnp.exp(l_a - l_b))

        def hidden(e_loc, gate):
            a = jnp.dot(x, wg_ref[e_loc], preferred_element_type=F32)
            u = jnp.dot(x, wu_ref[e_loc], preferred_element_type=F32)
            return ((a * jax.nn.sigmoid(a)) * u * gate).astype(BF16)

        a_loc, b_loc = ea % EXPERTS_PER_GROUP, eb % EXPERTS_PER_GROUP
        y = (jnp.dot(hidden(a_loc, gate_a), wd_ref[a_loc], preferred_element_type=F32)
             + jnp.dot(hidden(b_loc, gate_b), wd_ref[b_loc], preferred_element_type=F32))
        for c in range(SUBLANES):
            y_ref[pl.ds(c, TM_MOE, stride=SUBLANES), :] = y[:, c * LANES:(c + 1) * LANES]
        start_gather(i + GATHER_AHEAD)

    @pl.when(nv_ref[i] == 0)
    def _():
        y_ref[...] = jnp.zeros(y_ref.shape, F32)

        @pl.when((i < GATHER_AHEAD) | (nv_ref[jnp.maximum(i - GATHER_AHEAD, 0)] > 0))
        def _():
            wait_gather(i)


def _run_moe(key, row_off, tile_a, tile_b, tile_rows, tp8, w_r, b_r, wg, wu, wd):
    n_tiles = tile_rows.shape[0]
    n_tok = key.shape[0]
    assert n_tok & (n_tok - 1) == 0, "token count must be a power of two (class/rank key packing)"
    group = lambda i, k, o, ta, tb, nv: (ta[i] // EXPERTS_PER_GROUP, 0, 0)
    grid_spec = pltpu.PrefetchScalarGridSpec(
        num_scalar_prefetch=5,
        grid=(n_tiles,),
        in_specs=[
            pl.BlockSpec(memory_space=pl.ANY),
            pl.BlockSpec((D_MODEL, LANES), lambda i, k, o, ta, tb, nv: (0, 0)),
            pl.BlockSpec((1, LANES), lambda i, k, o, ta, tb, nv: (0, 0)),
            pl.BlockSpec((EXPERTS_PER_GROUP, D_MODEL, D_FF_EXPERT), group),
            pl.BlockSpec((EXPERTS_PER_GROUP, D_MODEL, D_FF_EXPERT), group),
            pl.BlockSpec((EXPERTS_PER_GROUP, D_FF_EXPERT, D_MODEL), group),
        ],
        out_specs=pl.BlockSpec((TILE_ROWS, LANES), lambda i, k, o, ta, tb, nv: (i, 0)),
        scratch_shapes=[
            pltpu.SMEM((n_tiles * TM_MOE,), jnp.int32),
            pltpu.VMEM((GATHER_SLOTS * TILE_ROWS, LANES), F32),
            pltpu.SemaphoreType.DMA((GATHER_SLOTS,)),
        ],
    )
    return pl.pallas_call(
        _moe_kernel,
        grid_spec=grid_spec,
        out_shape=jax.ShapeDtypeStruct((n_tiles * TILE_ROWS, LANES), F32),
        compiler_params=pltpu.CompilerParams(
            dimension_semantics=("arbitrary",), vmem_limit_bytes=VMEM_LIMIT),
        name="moe",
    )(key, row_off, tile_a, tile_b, tile_rows, tp8, w_r, b_r, wg, wu, wd)


FIN_ROWS = TM_FIN * SUBLANES


def _final_kernel(key_ref, off_ref, h_ref, y_hbm, g_ref, out_ref, ybuf, sem):
    i = pl.program_id(0)
    slot = i % 2

    def start_gather(tile, slot):
        def body(r, c):
            p = _sorted_row(key_ref, off_ref, tile * TM_FIN + r)
            pltpu.make_async_copy(
                y_hbm.at[pl.ds(pl.multiple_of(p * SUBLANES, SUBLANES), SUBLANES), :],
                ybuf.at[pl.ds(pl.multiple_of(slot * FIN_ROWS + r * SUBLANES, SUBLANES), SUBLANES), :],
                sem.at[slot]).start()
            return c
        lax.fori_loop(0, TM_FIN, body, 0, unroll=8)

    @pl.when(i == 0)
    def _():
        start_gather(0, 0)

    @pl.when(i + 1 < pl.num_programs(0))
    def _():
        start_gather(i + 1, 1 - slot)

    pltpu.make_async_copy(
        y_hbm.at[pl.ds(0, FIN_ROWS), :],
        ybuf.at[pl.ds(pl.multiple_of(slot * FIN_ROWS, SUBLANES), FIN_ROWS), :],
        sem.at[slot]).wait()
    base = slot * FIN_ROWS
    y = jnp.concatenate([ybuf[pl.ds(base + c, TM_FIN, stride=SUBLANES), :] for c in range(SUBLANES)], axis=1)
    out_ref[...] = _rms(h_ref[...] + y, g_ref[...])


def _run_final(key, row_off, h2, ys8, g_final):
    T = h2.shape[0]
    grid_spec = pltpu.PrefetchScalarGridSpec(
        num_scalar_prefetch=2,
        grid=(T // TM_FIN,),
        in_specs=[
            pl.BlockSpec((TM_FIN, D_MODEL), lambda i, k, o: (i, 0)),
            pl.BlockSpec(memory_space=pl.ANY),
            pl.BlockSpec((1, D_MODEL), lambda i, k, o: (0, 0)),
        ],
        out_specs=pl.BlockSpec((TM_FIN, D_MODEL), lambda i, k, o: (i, 0)),
        scratch_shapes=[pltpu.VMEM((2 * FIN_ROWS, LANES), F32), pltpu.SemaphoreType.DMA((2,))],
    )
    return pl.pallas_call(
        _final_kernel,
        grid_spec=grid_spec,
        out_shape=jax.ShapeDtypeStruct((T, D_MODEL), F32),
        compiler_params=pltpu.CompilerParams(
            dimension_semantics=("arbitrary",), vmem_limit_bytes=VMEM_LIMIT),
        name="final",
    )(key, row_off, h2, ys8, g_final)


def _prep_attention_weights(w_in, w_uq, w_ukv):
    d = w_in.shape[0]
    c0 = MLA_Q_RANK + MLA_KV_RANK
    kpe = w_in[:, c0:c0 + MLA_QK_ROPE]
    kpe_blk = jnp.concatenate(
        [jnp.zeros((d, MLA_QK_NOPE), F32), kpe, jnp.zeros((d, LANES - MLA_QK_NOPE - MLA_QK_ROPE), F32)], axis=1)
    w_in_r = jnp.concatenate([w_in[:, :c0], kpe_blk, w_in[:, c0 + MLA_QK_ROPE:]], axis=1).astype(BF16)

    dqk = MLA_QK_NOPE + MLA_QK_ROPE
    uq = w_uq.reshape(MLA_Q_RANK, MLA_HEADS, dqk)
    uq = jnp.pad(uq, ((0, 0), (0, 0), (0, LANES - dqk))).reshape(MLA_Q_RANK, MLA_HEADS * LANES).astype(BF16)

    ukv = w_ukv.reshape(MLA_KV_RANK, MLA_HEADS, MLA_QK_NOPE + MLA_V_DIM)
    uk = jnp.pad(ukv[:, :, :MLA_QK_NOPE], ((0, 0), (0, 0), (0, LANES - MLA_QK_NOPE)))
    uk = uk.reshape(MLA_KV_RANK, MLA_HEADS * LANES)
    uv = ukv[:, :, MLA_QK_NOPE:].reshape(MLA_KV_RANK, MLA_HEADS * MLA_V_DIM)
    w_ukv_r = jnp.concatenate([uk, uv], axis=1).astype(BF16)
    return w_in_r, uq, w_ukv_r


def _rope_freq_lanes():
    fb = ROPE_THETA ** (-jnp.arange(DIFF_HALF // 2, dtype=F32) / (DIFF_HALF // 2))
    fa = ROPE_THETA ** (-jnp.arange(MLA_QK_ROPE // 2, dtype=F32) / (MLA_QK_ROPE // 2))
    return jnp.concatenate([fb, fb, fa, fa, jnp.zeros((LANES - 96,), F32)])[None, :]


def kernel(x, positions, attn_norm_g, w_in, q_norm_g, w_uq, kv_norm_g, w_ukv, lambda_q1, lambda_k1, lambda_q2, lambda_k2, subln_g, w_o, ffn_norm_g, w_router_group, b_router_group, w_router_expert, b_router_expert, w_gate, w_up, w_down, final_norm_g):
    B, S, D = x.shape
    T = B * S
    l = 0

    w_in_r, w_uq_r, w_ukv_r = _prep_attention_weights(w_in[l], w_uq[l], w_ukv[l])
    pos_f = positions.astype(F32)[:, :, None]
    qa, ka, va = _run_proj(x, pos_f, _rope_freq_lanes(), attn_norm_g[l][None, :], w_in_r,
                           q_norm_g[l][None, :], w_uq_r, kv_norm_g[l][None, :], w_ukv_r)

    lam_vecs = jnp.stack([lambda_q1[l], lambda_k1[l], lambda_q2[l], lambda_k2[l]]).astype(F32)
    o = _run_attn(qa, ka, va, lam_vecs, subln_g[l][None, :])

    w_r = jnp.concatenate(
        [w_router_group[l], w_router_expert[l].reshape(D, N_EXPERTS),
         jnp.zeros((D, LANES - N_GROUPS - N_EXPERTS), F32)], axis=1).astype(BF16)
    b_r = jnp.concatenate(
        [b_router_group[l], b_router_expert[l].reshape(N_EXPERTS),
         jnp.zeros((LANES - N_GROUPS - N_EXPERTS,), F32)])[None, :].astype(F32)
    ii = jnp.arange(TM_POST)
    tri = (ii[None, :] < ii[:, None]).astype(BF16)
    x2 = x.reshape(T, D)
    h2, tp8, route, counts = _run_post(o.reshape(T, D), x2, w_o[l].astype(BF16), ffn_norm_g[l][None, :],
                                       w_r, b_r, tri)

    n_tiles_max = T // TM_MOE + N_CLASSES + GATHER_AHEAD - 1
    key = route[:, 0].astype(jnp.int32) * T + route[:, 1].astype(jnp.int32)
    cnt = counts[0, :N_CLASSES].astype(jnp.int32)
    tiles_c = (cnt + TM_MOE - 1) // TM_MOE
    tile_end = jnp.cumsum(tiles_c)
    n_tiles = tile_end[-1]
    row_off = jnp.concatenate([(tile_end - tiles_c) * TM_MOE, (n_tiles * TM_MOE)[None],
                               jnp.zeros((LANES - N_CLASSES - 1,), jnp.int32)])
    tile_ids = jnp.arange(n_tiles_max, dtype=jnp.int32)
    tile_cls = jnp.sum(jnp.minimum(tile_ids, n_tiles - 1)[:, None] >= tile_end[None, :], axis=1, dtype=jnp.int32)
    pair_a, pair_b = [], []
    for g in range(N_GROUPS):
        for a in range(EXPERTS_PER_GROUP):
            for b in range(a + 1, EXPERTS_PER_GROUP):
                pair_a.append(g * EXPERTS_PER_GROUP + a)
                pair_b.append(g * EXPERTS_PER_GROUP + b)
    cls_onehot = (tile_cls[:, None] == jnp.arange(N_CLASSES, dtype=jnp.int32)[None, :]).astype(jnp.int32)
    tile_a = cls_onehot @ jnp.array(pair_a, jnp.int32)
    tile_b = cls_onehot @ jnp.array(pair_b, jnp.int32)
    left = cls_onehot @ cnt - (tile_ids - cls_onehot @ (tile_end - tiles_c)) * TM_MOE
    tile_rows = jnp.where(tile_ids < n_tiles, jnp.clip(left, 0, TM_MOE), 0).astype(jnp.int32)

    ys8 = _run_moe(key, row_off, tile_a, tile_b, tile_rows, tp8, w_r, b_r,
                        w_gate[l].astype(BF16), w_up[l].astype(BF16), w_down[l].astype(BF16))
    out = _run_final(key, row_off, h2, ys8, final_norm_g[None, :])
    return out.reshape(B, S, D)
```

```python
import functools
import math

import jax
import jax.numpy as jnp
from jax import lax
from jax.experimental import pallas as pl
from jax.experimental.pallas import tpu as pltpu

F32 = jnp.float32
BF16 = jnp.bfloat16

D_MODEL = 1024
MLA_HEADS = 8
MLA_QK_NOPE = 64
MLA_QK_ROPE = 32
MLA_V_DIM = 64
MLA_Q_RANK = 384
MLA_KV_RANK = 256
DIFF_HEADS = 4
DIFF_HALF = 64
DIFF_V_DIM = 2 * DIFF_HALF
N_GROUPS = 4
EXPERTS_PER_GROUP = 8
N_EXPERTS = N_GROUPS * EXPERTS_PER_GROUP
D_FF_EXPERT = 256
ROPE_THETA = 10000.0
EPS = 1e-6
LAM_INIT = 0.8 - 0.6 * math.exp(-0.3 * 0)

LANES = 128
N_QK_BLOCKS = MLA_HEADS + DIFF_HEADS
N_V_BLOCKS = MLA_HEADS // 2 + DIFF_HEADS
N_UNITS = MLA_HEADS // 2 + DIFF_HEADS
PROJ_COLS = MLA_Q_RANK + MLA_KV_RANK + LANES + 3 * DIFF_HEADS * DIFF_V_DIM
LOG2E = 1.4426950408889634
NEG_BIG = -1e30

TM_PROJ = 512
TQ = 256
TK = 256
SCORE_LEAD = 6
TM_POST = 512
TM_MOE = 128
TM_FIN = 256
SUBLANES = 8
PAIRS_PER_GROUP = EXPERTS_PER_GROUP * (EXPERTS_PER_GROUP - 1) // 2
N_CLASSES = N_GROUPS * PAIRS_PER_GROUP
VMEM_LIMIT = 48 * 1024 * 1024


def _rms(x, g):
    return x * lax.rsqrt(jnp.mean(x * x, axis=-1, keepdims=True) + EPS) * g


def _proj_kernel(x_ref, pos_ref, freq_ref, g_attn_ref, w_in_ref, gq_ref, w_uq_ref, gkv_ref, w_ukv_ref,
                 qa_ref, ka_ref, va_ref):
    xn = _rms(x_ref[0], g_attn_ref[...]).astype(BF16)
    proj = jnp.dot(xn, w_in_ref[...], preferred_element_type=F32)

    ang = pos_ref[0] * freq_ref[...]
    cos_t, sin_t = jnp.cos(ang), jnp.sin(ang)
    lane = lax.broadcasted_iota(jnp.int32, ang.shape, 1)
    in_rope = (lane >= 64) & (lane < 96)
    cos_a = jnp.where(in_rope, cos_t, 1.0)
    sin_a_hi = jnp.where((lane >= 80) & (lane < 96), sin_t, 0.0)
    sin_a_lo = jnp.where((lane >= 64) & (lane < 80), -sin_t, 0.0)
    cos_b = jnp.where(lane < 64, cos_t, pltpu.roll(cos_t, 64, 1))
    sin_b = jnp.where(lane < 64, sin_t, pltpu.roll(sin_t, 64, 1))
    sub = lane % 64
    sin_b_hi = jnp.where(sub >= 32, sin_b, 0.0)
    sin_b_lo = jnp.where(sub < 32, -sin_b, 0.0)

    def rope_a(xb, scale):
        return (xb * (cos_a * scale) + pltpu.roll(xb, 16, 1) * (sin_a_hi * scale)
                + pltpu.roll(xb, 112, 1) * (sin_a_lo * scale))

    def rope_b(xb, scale):
        return (xb * (cos_b * scale) + pltpu.roll(xb, 32, 1) * (sin_b_hi * scale)
                + pltpu.roll(xb, 96, 1) * (sin_b_lo * scale))

    c_q = proj[:, :MLA_Q_RANK]
    c_kv = proj[:, MLA_Q_RANK:MLA_Q_RANK + MLA_KV_RANK]
    base = MLA_Q_RANK + MLA_KV_RANK
    kpe = rope_a(proj[:, base:base + LANES], 1.0)
    base += LANES

    q = jnp.dot(_rms(c_q, gq_ref[...]).astype(BF16), w_uq_ref[...], preferred_element_type=F32)
    kv = jnp.dot(_rms(c_kv, gkv_ref[...]).astype(BF16), w_ukv_ref[...], preferred_element_type=F32)

    scale_mla = (MLA_QK_NOPE + MLA_QK_ROPE) ** -0.5 * LOG2E
    scale_diff = DIFF_HALF ** -0.5 * LOG2E
    for h in range(MLA_HEADS):
        sl = slice(h * LANES, (h + 1) * LANES)
        qa_ref[0, h] = rope_a(q[:, sl], scale_mla).astype(BF16)
        ka_ref[0, h] = (kv[:, sl] + kpe).astype(BF16)
    vbase = MLA_HEADS * LANES
    for hp in range(MLA_HEADS // 2):
        va_ref[0, hp] = kv[:, vbase + hp * LANES: vbase + (hp + 1) * LANES].T.astype(BF16)
    width = DIFF_HEADS * DIFF_V_DIM
    for h in range(DIFF_HEADS):
        sl = slice(base + h * LANES, base + (h + 1) * LANES)
        qa_ref[0, MLA_HEADS + h] = rope_b(proj[:, sl], scale_diff).astype(BF16)
        sl = slice(base + width + h * LANES, base + width + (h + 1) * LANES)
        ka_ref[0, MLA_HEADS + h] = rope_b(proj[:, sl], 1.0).astype(BF16)
        sl = slice(base + 2 * width + h * LANES, base + 2 * width + (h + 1) * LANES)
        va_ref[0, MLA_HEADS // 2 + h] = proj[:, sl].T.astype(BF16)


def _run_proj(x, pos_f, freq, g_attn, w_in_r, gq, w_uq_r, gkv, w_ukv_r):
    B, S, _ = x.shape
    tm = TM_PROJ
    full = lambda shape: pl.BlockSpec(shape, lambda b, i: (0,) * len(shape))
    head_out = lambda n: pl.BlockSpec((1, n, tm, LANES), lambda b, i: (b, 0, i, 0))
    return pl.pallas_call(
        _proj_kernel,
        grid=(B, S // tm),
        in_specs=[
            pl.BlockSpec((1, tm, D_MODEL), lambda b, i: (b, i, 0)),
            pl.BlockSpec((1, tm, 1), lambda b, i: (b, i, 0)),
            full((1, LANES)),
            full((1, D_MODEL)),
            full((D_MODEL, PROJ_COLS)),
            full((1, MLA_Q_RANK)),
            full((MLA_Q_RANK, MLA_HEADS * LANES)),
            full((1, MLA_KV_RANK)),
            full((MLA_KV_RANK, MLA_HEADS * LANES + MLA_HEADS * MLA_V_DIM)),
        ],
        out_specs=[head_out(N_QK_BLOCKS), head_out(N_QK_BLOCKS),
                   pl.BlockSpec((1, N_V_BLOCKS, LANES, tm), lambda b, i: (b, 0, 0, i))],
        out_shape=[
            jax.ShapeDtypeStruct((B, N_QK_BLOCKS, S, LANES), BF16),
            jax.ShapeDtypeStruct((B, N_QK_BLOCKS, S, LANES), BF16),
            jax.ShapeDtypeStruct((B, N_V_BLOCKS, LANES, S), BF16),
        ],
        compiler_params=pltpu.CompilerParams(
            dimension_semantics=("arbitrary", "arbitrary"), vmem_limit_bytes=VMEM_LIMIT),
        name="proj",
    )(x, pos_f, freq, g_attn, w_in_r, gq, w_uq_r, gkv, w_ukv_r)


def _attn_kernel(qa_ref, ka_ref, vt_ref, lam_ref, subln_ref, o_ref, m_ref, l_ref, acc_ref):
    qi = pl.program_id(1)
    m_ref[...] = jnp.full(m_ref.shape, NEG_BIG, F32)
    l_ref[...] = jnp.zeros(l_ref.shape, F32)
    acc_ref[...] = jnp.zeros(acc_ref.shape, F32)

    low_half = lax.broadcasted_iota(jnp.int32, (TQ, LANES), 1) < 64
    is_mla = lambda u: u < MLA_HEADS // 2
    v_rows = lambda u, s: slice(s * MLA_V_DIM, (s + 1) * MLA_V_DIM) if is_mla(u) else slice(0, DIFF_V_DIM)

    def unit_operands(u):
        if is_mla(u):
            return (qa_ref[0, 2 * u], 2 * u), (qa_ref[0, 2 * u + 1], 2 * u + 1), u
        h = u - MLA_HEADS // 2
        q = qa_ref[0, MLA_HEADS + h]
        zero = jnp.zeros_like(q)
        return ((jnp.where(low_half, q, zero), MLA_HEADS + h),
                (jnp.where(low_half, zero, q), MLA_HEADS + h), MLA_HEADS // 2 + h)

    def scores(q, k, mask):
        st = lax.dot_general(k, q, (((1,), (1,)), ((), ())), preferred_element_type=F32)
        return st if mask is None else jnp.where(mask, st, NEG_BIG)

    def softmax_update(slot, rows, st, vt):
        m_prev = m_ref[slot]
        m_new = jnp.maximum(m_prev, jnp.max(st, axis=0, keepdims=True))
        alpha = jnp.exp2(m_prev - m_new)
        pt = jnp.exp2(st - m_new)
        l_ref[slot] = alpha * l_ref[slot] + jnp.sum(pt, axis=0, keepdims=True)
        acc_ref[slot, rows] = alpha * acc_ref[slot, rows] + jnp.dot(vt[rows], pt.astype(BF16),
                                                                    preferred_element_type=F32)
        m_ref[slot] = m_new

    streams = []
    for u in range(N_UNITS):
        (q0, k0), (q1, k1), vi = unit_operands(u)
        streams += [(2 * u, v_rows(u, 0), q0, k0, vi), (2 * u + 1, v_rows(u, 1), q1, k1, vi)]

    def kv_step(j, mask):
        keys = pl.ds(pl.multiple_of(j * TK, TK), TK)
        pending = {}
        for n in range(len(streams) + SCORE_LEAD):
            if n < len(streams):
                _, _, q, kidx, _ = streams[n]
                pending[n] = scores(q, ka_ref[0, kidx, keys, :], mask)
            if n >= SCORE_LEAD:
                slot, rows, _, _, vi = streams[n - SCORE_LEAD]
                softmax_update(slot, rows, pending.pop(n - SCORE_LEAD), vt_ref[0, vi, :, keys])

    def body(j, carry):
        kv_step(j, None)
        return carry

    lax.fori_loop(0, qi, body, 0)
    key_pos = lax.broadcasted_iota(jnp.int32, (TK, TQ), 0)
    query_pos = lax.broadcasted_iota(jnp.int32, (TK, TQ), 1)
    kv_step(qi, key_pos <= query_pos)

    lam_v = lam_ref[...]
    lam = (jnp.exp(jnp.sum(lam_v[0:1] * lam_v[1:2], axis=1, keepdims=True))
           - jnp.exp(jnp.sum(lam_v[2:3] * lam_v[3:4], axis=1, keepdims=True)) + LAM_INIT)
    for u in range(N_UNITS):
        r0, r1 = v_rows(u, 0), v_rows(u, 1)
        o0 = acc_ref[2 * u, r0] / l_ref[2 * u]
        o1 = acc_ref[2 * u + 1, r1] / l_ref[2 * u + 1]
        if is_mla(u):
            out = jnp.concatenate([o0, o1], axis=0).T
            col0 = u * LANES
        else:
            d = o0 - lam * o1
            d = d * lax.rsqrt(jnp.mean(d * d, axis=0, keepdims=True) + EPS)
            out = d.T * subln_ref[...] * (1.0 - LAM_INIT)
            col0 = MLA_HEADS * MLA_V_DIM + (u - MLA_HEADS // 2) * LANES
        o_ref[0, :, col0:col0 + LANES] = out.astype(BF16)


def _run_attn(qa, ka, vt, lam_vecs, subln):
    B, _, S, _ = qa.shape
    n_slots = 2 * N_UNITS
    return pl.pallas_call(
        _attn_kernel,
        grid=(B, S // TQ),
        in_specs=[
            pl.BlockSpec((1, N_QK_BLOCKS, TQ, LANES), lambda b, i: (b, 0, i, 0)),
            pl.BlockSpec((1, N_QK_BLOCKS, S, LANES), lambda b, i: (b, 0, 0, 0)),
            pl.BlockSpec((1, N_V_BLOCKS, LANES, S), lambda b, i: (b, 0, 0, 0)),
            pl.BlockSpec((4, DIFF_HALF), lambda b, i: (0, 0)),
            pl.BlockSpec((1, DIFF_V_DIM), lambda b, i: (0, 0)),
        ],
        out_specs=pl.BlockSpec((1, TQ, D_MODEL), lambda b, i: (b, i, 0)),
        out_shape=jax.ShapeDtypeStruct((B, S, D_MODEL), BF16),
        scratch_shapes=[
            pltpu.VMEM((n_slots, 1, TQ), F32),
            pltpu.VMEM((n_slots, 1, TQ), F32),
            pltpu.VMEM((n_slots, LANES, TQ), F32),
        ],
        compiler_params=pltpu.CompilerParams(
            dimension_semantics=("arbitrary", "arbitrary"), vmem_limit_bytes=VMEM_LIMIT),
        name="attn",
    )(qa, ka, vt, lam_vecs, subln)


def _post_kernel(o_ref, x_ref, w_o_ref, g_ffn_ref, w_r_ref, b_r_ref, tri_ref,
                 h_ref, tp_ref, route_ref, cnt_ref):
    tm = x_ref.shape[0]

    @pl.when(pl.program_id(0) == 0)
    def _():
        cnt_ref[...] = jnp.zeros(cnt_ref.shape, F32)

    h = x_ref[...] + jnp.dot(o_ref[...], w_o_ref[...], preferred_element_type=F32)
    h_ref[...] = h
    t = _rms(h, g_ffn_ref[...])
    lg = jnp.dot(t.astype(BF16), w_r_ref[...], preferred_element_type=F32) + b_r_ref[...]

    lane = lax.broadcasted_iota(jnp.int32, lg.shape, 1)
    lane_f = lane.astype(F32)
    first_lane = lambda hit: jnp.min(jnp.where(hit, lane_f, float(LANES)), axis=1, keepdims=True)

    is_group = lane < N_GROUPS
    gl = jnp.where(is_group, lg, NEG_BIG)
    gmax = jnp.max(gl, axis=1, keepdims=True)
    gidx = first_lane(gl == gmax)

    lo = N_GROUPS + EXPERTS_PER_GROUP * gidx
    el = jnp.where((lane_f >= lo) & (lane_f < lo + EXPERTS_PER_GROUP), lg, NEG_BIG)
    m1 = jnp.max(el, axis=1, keepdims=True)
    i1 = first_lane(el == m1)
    el2 = jnp.where(lane_f == i1, NEG_BIG, el)
    m2 = jnp.max(el2, axis=1, keepdims=True)
    i2 = first_lane(el2 == m2)

    a = jnp.minimum(i1, i2) - lo
    b = jnp.maximum(i1, i2) - lo
    pair = a * (2 * EXPERTS_PER_GROUP - 1 - a) * 0.5 + (b - a - 1.0)
    cls = gidx * PAIRS_PER_GROUP + pair

    hit = lane_f == cls
    onehot = jnp.where(hit, 1.0, 0.0)
    before = jnp.dot(tri_ref[...], onehot.astype(BF16), preferred_element_type=F32) + cnt_ref[...]
    rank = jnp.sum(jnp.where(hit, before, 0.0), axis=1, keepdims=True)
    cnt_ref[...] += jnp.sum(onehot, axis=0, keepdims=True)

    out = jnp.zeros(lg.shape, F32)
    for k, val in enumerate((cls, rank)):
        out = jnp.where(lane == k, val, out)
    route_ref[...] = out

    for c in range(SUBLANES):
        tp_ref[pl.ds(c, tm, stride=SUBLANES), :] = t[:, c * LANES:(c + 1) * LANES]


def _run_post(o2, x2, w_o, g_ffn, w_r, b_r, tri):
    T = x2.shape[0]
    tm = TM_POST
    full = lambda shape: pl.BlockSpec(shape, lambda i: (0,) * len(shape))
    rows = lambda w: pl.BlockSpec((tm, w), lambda i: (i, 0))
    return pl.pallas_call(
        _post_kernel,
        grid=(T // tm,),
        in_specs=[rows(D_MODEL), rows(D_MODEL), full((D_MODEL, D_MODEL)), full((1, D_MODEL)),
                  full((D_MODEL, LANES)), full((1, LANES)), full((tm, tm))],
        out_specs=[rows(D_MODEL), pl.BlockSpec((tm * SUBLANES, LANES), lambda i: (i, 0)), rows(LANES),
                   full((1, LANES))],
        out_shape=[
            jax.ShapeDtypeStruct((T, D_MODEL), F32),
            jax.ShapeDtypeStruct((T * SUBLANES, LANES), F32),
            jax.ShapeDtypeStruct((T, LANES), F32),
            jax.ShapeDtypeStruct((1, LANES), F32),
        ],
        compiler_params=pltpu.CompilerParams(
            dimension_semantics=("arbitrary",), vmem_limit_bytes=VMEM_LIMIT),
        name="post",
    )(o2, x2, w_o, g_ffn, w_r, b_r, tri)


TILE_ROWS = TM_MOE * SUBLANES


GATHER_AHEAD = 2
GATHER_SLOTS = GATHER_AHEAD + 1


def _sorted_row(key_ref, off_ref, t):
    n_tok = key_ref.shape[0]
    key = key_ref[t]
    return off_ref[key >> (n_tok.bit_length() - 1)] + (key & (n_tok - 1))


def _moe_kernel(key_ref, off_ref, ta_ref, tb_ref, nv_ref, tp_hbm, w_r_ref, b_r_ref, wg_ref, wu_ref, wd_ref,
                y_ref, src_ref, xbuf, sem):
    i = pl.program_id(0)
    n_tok = key_ref.shape[0]

    def start_gather(tile):
        slot = tile % GATHER_SLOTS
        for r in range(TM_MOE):
            tok = src_ref[tile * TM_MOE + r]
            pltpu.make_async_copy(
                tp_hbm.at[pl.ds(pl.multiple_of(tok * SUBLANES, SUBLANES), SUBLANES), :],
                xbuf.at[pl.ds(pl.multiple_of(slot * TILE_ROWS + r * SUBLANES, SUBLANES), SUBLANES), :],
                sem.at[slot]).start(priority=r % 2)

    def wait_gather(tile):
        slot = tile % GATHER_SLOTS
        pltpu.make_async_copy(
            tp_hbm.at[pl.ds(0, TILE_ROWS), :],
            xbuf.at[pl.ds(pl.multiple_of(slot * TILE_ROWS, SUBLANES), TILE_ROWS), :],
            sem.at[slot]).wait()

    @pl.when(i == 0)
    def _():
        def fill_pad(tile, c):
            def body(r, c2):
                src_ref[tile * TM_MOE + r] = (tile * TM_MOE + r) & (n_tok - 1)
                return c2
            return lax.fori_loop(nv_ref[tile], TM_MOE, body, c)

        def invert(t, c):
            src_ref[_sorted_row(key_ref, off_ref, t)] = t
            return c

        n_used = off_ref[N_CLASSES] // TM_MOE
        lax.fori_loop(0, n_used + GATHER_AHEAD, fill_pad, 0)
        lax.fori_loop(0, n_tok, invert, 0, unroll=16)
        for tile in range(GATHER_AHEAD):
            start_gather(tile)

    @pl.when(nv_ref[i] > 0)
    def _():
        wait_gather(i)
        base = (i % GATHER_SLOTS) * TILE_ROWS
        x = jnp.concatenate([xbuf[pl.ds(base + c, TM_MOE, stride=SUBLANES), :] for c in range(SUBLANES)],
                            axis=1).astype(BF16)

        lg = jnp.dot(x, w_r_ref[...], preferred_element_type=F32) + b_r_ref[...]
        lane = lax.broadcasted_iota(jnp.int32, lg.shape, 1)
        pick = lambda idx: jnp.sum(jnp.where(lane == idx, lg, 0.0), axis=1, keepdims=True)
        ea, eb = ta_ref[i], tb_ref[i]
        g_logit = pick(ea // EXPERTS_PER_GROUP)
        p_group = 1.0 / jnp.sum(jnp.where(lane < N_GROUPS, jnp.exp(lg - g_logit), 0.0), axis=1, keepdims=True)
        l_a, l_b = pick(N_GROUPS + ea), pick(N_GROUPS + eb)
        gate_a = p_group / (1.0 + jnp.exp(l_b - l_a))
        gate_b = p_group / (1.0 + jnp.exp(l_a - l_b))

        def hidden(e_loc, gate):
            a = jnp.dot(x, wg_ref[e_loc], preferred_element_type=F32)
            u = jnp.dot(x, wu_ref[e_loc], preferred_element_type=F32)
            return ((a * jax.nn.sigmoid(a)) * u * gate).astype(BF16)

        a_loc, b_loc = ea % EXPERTS_PER_GROUP, eb % EXPERTS_PER_GROUP
        y = (jnp.dot(hidden(a_loc, gate_a), wd_ref[a_loc], preferred_element_type=F32)
             + jnp.dot(hidden(b_loc, gate_b), wd_ref[b_loc], preferred_element_type=F32))
        for c in range(SUBLANES):
            y_ref[pl.ds(c, TM_MOE, stride=SUBLANES), :] = y[:, c * LANES:(c + 1) * LANES]
        start_gather(i + GATHER_AHEAD)

    @pl.when(nv_ref[i] == 0)
    def _():
        y_ref[...] = jnp.zeros(y_ref.shape, F32)

        @pl.when((i < GATHER_AHEAD) | (nv_ref[jnp.maximum(i - GATHER_AHEAD, 0)] > 0))
        def _():
            wait_gather(i)


def _run_moe(key, row_off, tile_a, tile_b, tile_rows, tp8, w_r, b_r, wg, wu, wd):
    n_tiles = tile_rows.shape[0]
    n_tok = key.shape[0]
    assert n_tok & (n_tok - 1) == 0, "token count must be a power of two (class/rank key packing)"
    group = lambda i, k, o, ta, tb, nv: (ta[i] // EXPERTS_PER_GROUP, 0, 0)
    grid_spec = pltpu.PrefetchScalarGridSpec(
        num_scalar_prefetch=5,
        grid=(n_tiles,),
        in_specs=[
            pl.BlockSpec(memory_space=pl.ANY),
            pl.BlockSpec((D_MODEL, LANES), lambda i, k, o, ta, tb, nv: (0, 0)),
            pl.BlockSpec((1, LANES), lambda i, k, o, ta, tb, nv: (0, 0)),
            pl.BlockSpec((EXPERTS_PER_GROUP, D_MODEL, D_FF_EXPERT), group),
            pl.BlockSpec((EXPERTS_PER_GROUP, D_MODEL, D_FF_EXPERT), group),
            pl.BlockSpec((EXPERTS_PER_GROUP, D_FF_EXPERT, D_MODEL), group),
        ],
        out_specs=pl.BlockSpec((TILE_ROWS, LANES), lambda i, k, o, ta, tb, nv: (i, 0)),
        scratch_shapes=[
            pltpu.SMEM((n_tiles * TM_MOE,), jnp.int32),
            pltpu.VMEM((GATHER_SLOTS * TILE_ROWS, LANES), F32),
            pltpu.SemaphoreType.DMA((GATHER_SLOTS,)),
        ],
    )
    return pl.pallas_call(
        _moe_kernel,
        grid_spec=grid_spec,
        out_shape=jax.ShapeDtypeStruct((n_tiles * TILE_ROWS, LANES), F32),
        compiler_params=pltpu.CompilerParams(
            dimension_semantics=("arbitrary",), vmem_limit_bytes=VMEM_LIMIT),
        name="moe",
    )(key, row_off, tile_a, tile_b, tile_rows, tp8, w_r, b_r, wg, wu, wd)


FIN_ROWS = TM_FIN * SUBLANES


def _final_kernel(key_ref, off_ref, h_ref, y_hbm, g_ref, out_ref, ybuf, sem):
    i = pl.program_id(0)
    slot = i % 2

    def start_gather(tile, slot):
        def body(r2, c):
            for q in range(2):
                r = 2 * r2 + q
                p = _sorted_row(key_ref, off_ref, tile * TM_FIN + r)
                pltpu.make_async_copy(
                    y_hbm.at[pl.ds(pl.multiple_of(p * SUBLANES, SUBLANES), SUBLANES), :],
                    ybuf.at[pl.ds(pl.multiple_of(slot * FIN_ROWS + r * SUBLANES, SUBLANES), SUBLANES), :],
                    sem.at[slot]).start(priority=q)
            return c
        lax.fori_loop(0, TM_FIN // 2, body, 0, unroll=4)

    @pl.when(i == 0)
    def _():
        start_gather(0, 0)

    @pl.when(i + 1 < pl.num_programs(0))
    def _():
        start_gather(i + 1, 1 - slot)

    pltpu.make_async_copy(
        y_hbm.at[pl.ds(0, FIN_ROWS), :],
        ybuf.at[pl.ds(pl.multiple_of(slot * FIN_ROWS, SUBLANES), FIN_ROWS), :],
        sem.at[slot]).wait()
    base = slot * FIN_ROWS
    y = jnp.concatenate([ybuf[pl.ds(base + c, TM_FIN, stride=SUBLANES), :] for c in range(SUBLANES)], axis=1)
    out_ref[...] = _rms(h_ref[...] + y, g_ref[...])


def _run_final(key, row_off, h2, ys8, g_final):
    T = h2.shape[0]
    grid_spec = pltpu.PrefetchScalarGridSpec(
        num_scalar_prefetch=2,
        grid=(T // TM_FIN,),
        in_specs=[
            pl.BlockSpec((TM_FIN, D_MODEL), lambda i, k, o: (i, 0)),
            pl.BlockSpec(memory_space=pl.ANY),
            pl.BlockSpec((1, D_MODEL), lambda i, k, o: (0, 0)),
        ],
        out_specs=pl.BlockSpec((TM_FIN, D_MODEL), lambda i, k, o: (i, 0)),
        scratch_shapes=[pltpu.VMEM((2 * FIN_ROWS, LANES), F32), pltpu.SemaphoreType.DMA((2,))],
    )
    return pl.pallas_call(
        _final_kernel,
        grid_spec=grid_spec,
        out_shape=jax.ShapeDtypeStruct((T, D_MODEL), F32),
        compiler_params=pltpu.CompilerParams(
            dimension_semantics=("arbitrary",), vmem_limit_bytes=VMEM_LIMIT),
        name="final",
    )(key, row_off, h2, ys8, g_final)


def _prep_attention_weights(w_in, w_uq, w_ukv):
    d = w_in.shape[0]
    c0 = MLA_Q_RANK + MLA_KV_RANK
    kpe = w_in[:, c0:c0 + MLA_QK_ROPE]
    kpe_blk = jnp.concatenate(
        [jnp.zeros((d, MLA_QK_NOPE), F32), kpe, jnp.zeros((d, LANES - MLA_QK_NOPE - MLA_QK_ROPE), F32)], axis=1)
    w_in_r = jnp.concatenate([w_in[:, :c0], kpe_blk, w_in[:, c0 + MLA_QK_ROPE:]], axis=1).astype(BF16)

    dqk = MLA_QK_NOPE + MLA_QK_ROPE
    uq = w_uq.reshape(MLA_Q_RANK, MLA_HEADS, dqk)
    uq = jnp.pad(uq, ((0, 0), (0, 0), (0, LANES - dqk))).reshape(MLA_Q_RANK, MLA_HEADS * LANES).astype(BF16)

    ukv = w_ukv.reshape(MLA_KV_RANK, MLA_HEADS, MLA_QK_NOPE + MLA_V_DIM)
    uk = jnp.pad(ukv[:, :, :MLA_QK_NOPE], ((0, 0), (0, 0), (0, LANES - MLA_QK_NOPE)))
    uk = uk.reshape(MLA_KV_RANK, MLA_HEADS * LANES)
    uv = ukv[:, :, MLA_QK_NOPE:].reshape(MLA_KV_RANK, MLA_HEADS * MLA_V_DIM)
    w_ukv_r = jnp.concatenate([uk, uv], axis=1).astype(BF16)
    return w_in_r, uq, w_ukv_r


def _rope_freq_lanes():
    fb = ROPE_THETA ** (-jnp.arange(DIFF_HALF // 2, dtype=F32) / (DIFF_HALF // 2))
    fa = ROPE_THETA ** (-jnp.arange(MLA_QK_ROPE // 2, dtype=F32) / (MLA_QK_ROPE // 2))
    return jnp.concatenate([fb, fb, fa, fa, jnp.zeros((LANES - 96,), F32)])[None, :]


def kernel(x, positions, attn_norm_g, w_in, q_norm_g, w_uq, kv_norm_g, w_ukv, lambda_q1, lambda_k1, lambda_q2, lambda_k2, subln_g, w_o, ffn_norm_g, w_router_group, b_router_group, w_router_expert, b_router_expert, w_gate, w_up, w_down, final_norm_g):
    B, S, D = x.shape
    T = B * S
    l = 0

    w_in_r, w_uq_r, w_ukv_r = _prep_attention_weights(w_in[l], w_uq[l], w_ukv[l])
    pos_f = positions.astype(F32)[:, :, None]
    qa, ka, va = _run_proj(x, pos_f, _rope_freq_lanes(), attn_norm_g[l][None, :], w_in_r,
                           q_norm_g[l][None, :], w_uq_r, kv_norm_g[l][None, :], w_ukv_r)

    lam_vecs = jnp.stack([lambda_q1[l], lambda_k1[l], lambda_q2[l], lambda_k2[l]]).astype(F32)
    o = _run_attn(qa, ka, va, lam_vecs, subln_g[l][None, :])

    w_r = jnp.concatenate(
        [w_router_group[l], w_router_expert[l].reshape(D, N_EXPERTS),
         jnp.zeros((D, LANES - N_GROUPS - N_EXPERTS), F32)], axis=1).astype(BF16)
    b_r = jnp.concatenate(
        [b_router_group[l], b_router_expert[l].reshape(N_EXPERTS),
         jnp.zeros((LANES - N_GROUPS - N_EXPERTS,), F32)])[None, :].astype(F32)
    ii = jnp.arange(TM_POST)
    tri = (ii[None, :] < ii[:, None]).astype(BF16)
    x2 = x.reshape(T, D)
    h2, tp8, route, counts = _run_post(o.reshape(T, D), x2, w_o[l].astype(BF16), ffn_norm_g[l][None, :],
                                       w_r, b_r, tri)

    n_tiles_max = T // TM_MOE + N_CLASSES + GATHER_AHEAD - 1
    key = route[:, 0].astype(jnp.int32) * T + route[:, 1].astype(jnp.int32)
    cnt = counts[0, :N_CLASSES].astype(jnp.int32)
    tiles_c = (cnt + TM_MOE - 1) // TM_MOE
    tile_end = jnp.cumsum(tiles_c)
    n_tiles = tile_end[-1]
    row_off = jnp.concatenate([(tile_end - tiles_c) * TM_MOE, (n_tiles * TM_MOE)[None],
                               jnp.zeros((LANES - N_CLASSES - 1,), jnp.int32)])
    tile_ids = jnp.arange(n_tiles_max, dtype=jnp.int32)
    tile_cls = jnp.sum(jnp.minimum(tile_ids, n_tiles - 1)[:, None] >= tile_end[None, :], axis=1, dtype=jnp.int32)
    pair_a, pair_b = [], []
    for g in range(N_GROUPS):
        for a in range(EXPERTS_PER_GROUP):
            for b in range(a + 1, EXPERTS_PER_GROUP):
                pair_a.append(g * EXPERTS_PER_GROUP + a)
                pair_b.append(g * EXPERTS_PER_GROUP + b)
    cls_onehot = (tile_cls[:, None] == jnp.arange(N_CLASSES, dtype=jnp.int32)[None, :]).astype(jnp.int32)
    tile_a = cls_onehot @ jnp.array(pair_a, jnp.int32)
    tile_b = cls_onehot @ jnp.array(pair_b, jnp.int32)
    left = cls_onehot @ cnt - (tile_ids - cls_onehot @ (tile_end - tiles_c)) * TM_MOE
    tile_rows = jnp.where(tile_ids < n_tiles, jnp.clip(left, 0, TM_MOE), 0).astype(jnp.int32)

    ys8 = _run_moe(key, row_off, tile_a, tile_b, tile_rows, tp8, w_r, b_r,
                        w_gate[l].astype(BF16), w_up[l].astype(BF16), w_down[l].astype(BF16))
    out = _run_final(key, row_off, h2, ys8, final_norm_g[None, :])
    return out.reshape(B, S, D)
```

```python
import functools
import math

import jax
import jax.numpy as jnp
from jax import lax
from jax.experimental import pallas as pl
from jax.experimental.pallas import tpu as pltpu

F32 = jnp.float32
BF16 = jnp.bfloat16

D_MODEL = 1024
MLA_HEADS = 8
MLA_QK_NOPE = 64
MLA_QK_ROPE = 32
MLA_V_DIM = 64
MLA_Q_RANK = 384
MLA_KV_RANK = 256
DIFF_HEADS = 4
DIFF_HALF = 64
DIFF_V_DIM = 2 * DIFF_HALF
N_GROUPS = 4
EXPERTS_PER_GROUP = 8
N_EXPERTS = N_GROUPS * EXPERTS_PER_GROUP
D_FF_EXPERT = 256
ROPE_THETA = 10000.0
EPS = 1e-6
LAM_INIT = 0.8 - 0.6 * math.exp(-0.3 * 0)

LANES = 128
N_QK_BLOCKS = MLA_HEADS + DIFF_HEADS
N_V_BLOCKS = MLA_HEADS // 2 + DIFF_HEADS
N_UNITS = MLA_HEADS // 2 + DIFF_HEADS
PROJ_COLS = MLA_Q_RANK + MLA_KV_RANK + LANES + 3 * DIFF_HEADS * DIFF_V_DIM
LOG2E = 1.4426950408889634
NEG_BIG = -1e30

TM_PROJ = 512
PROJ_SPLIT = 2
TQ = 256
TK = 256
SCORE_LEAD = 6
TM_POST = 512
TM_MOE = 128
TM_FIN = 256
SUBLANES = 8
PAIRS_PER_GROUP = EXPERTS_PER_GROUP * (EXPERTS_PER_GROUP - 1) // 2
N_CLASSES = N_GROUPS * PAIRS_PER_GROUP
VMEM_LIMIT = 48 * 1024 * 1024


def _rms(x, g):
    return x * lax.rsqrt(jnp.mean(x * x, axis=-1, keepdims=True) + EPS) * g


def _proj_kernel(x_ref, pos_ref, freq_ref, g_attn_ref, w_in_ref, gq_ref, w_uq_ref, gkv_ref, w_ukv_ref,
                 qa_ref, ka_ref, va_ref):
    tm = x_ref.shape[1] // PROJ_SPLIT
    for part in range(PROJ_SPLIT):
        _proj_rows(slice(part * tm, (part + 1) * tm), x_ref, pos_ref, freq_ref, g_attn_ref, w_in_ref, gq_ref,
                   w_uq_ref, gkv_ref, w_ukv_ref, qa_ref, ka_ref, va_ref)


def _proj_rows(rows, x_ref, pos_ref, freq_ref, g_attn_ref, w_in_ref, gq_ref, w_uq_ref, gkv_ref, w_ukv_ref,
               qa_ref, ka_ref, va_ref):
    xn = _rms(x_ref[0, rows], g_attn_ref[...]).astype(BF16)
    proj = jnp.dot(xn, w_in_ref[...], preferred_element_type=F32)

    ang = pos_ref[0, rows] * freq_ref[...]
    cos_t, sin_t = jnp.cos(ang), jnp.sin(ang)
    lane = lax.broadcasted_iota(jnp.int32, ang.shape, 1)
    in_rope = (lane >= 64) & (lane < 96)
    cos_a = jnp.where(in_rope, cos_t, 1.0)
    sin_a_hi = jnp.where((lane >= 80) & (lane < 96), sin_t, 0.0)
    sin_a_lo = jnp.where((lane >= 64) & (lane < 80), -sin_t, 0.0)
    cos_b = jnp.where(lane < 64, cos_t, pltpu.roll(cos_t, 64, 1))
    sin_b = jnp.where(lane < 64, sin_t, pltpu.roll(sin_t, 64, 1))
    sub = lane % 64
    sin_b_hi = jnp.where(sub >= 32, sin_b, 0.0)
    sin_b_lo = jnp.where(sub < 32, -sin_b, 0.0)

    def rope_a(xb, scale):
        return (xb * (cos_a * scale) + pltpu.roll(xb, 16, 1) * (sin_a_hi * scale)
                + pltpu.roll(xb, 112, 1) * (sin_a_lo * scale))

    def rope_b(xb, scale):
        return (xb * (cos_b * scale) + pltpu.roll(xb, 32, 1) * (sin_b_hi * scale)
                + pltpu.roll(xb, 96, 1) * (sin_b_lo * scale))

    c_q = proj[:, :MLA_Q_RANK]
    c_kv = proj[:, MLA_Q_RANK:MLA_Q_RANK + MLA_KV_RANK]
    base = MLA_Q_RANK + MLA_KV_RANK
    kpe = rope_a(proj[:, base:base + LANES], 1.0)
    base += LANES

    q = jnp.dot(_rms(c_q, gq_ref[...]).astype(BF16), w_uq_ref[...], preferred_element_type=F32)
    kv = jnp.dot(_rms(c_kv, gkv_ref[...]).astype(BF16), w_ukv_ref[...], preferred_element_type=F32)

    scale_mla = (MLA_QK_NOPE + MLA_QK_ROPE) ** -0.5 * LOG2E
    scale_diff = DIFF_HALF ** -0.5 * LOG2E
    for h in range(MLA_HEADS):
        sl = slice(h * LANES, (h + 1) * LANES)
        qa_ref[0, h, rows] = rope_a(q[:, sl], scale_mla).astype(BF16)
        ka_ref[0, h, rows] = (kv[:, sl] + kpe).astype(BF16)
    vbase = MLA_HEADS * LANES
    for hp in range(MLA_HEADS // 2):
        va_ref[0, hp, :, rows] = kv[:, vbase + hp * LANES: vbase + (hp + 1) * LANES].T.astype(BF16)
    width = DIFF_HEADS * DIFF_V_DIM
    for h in range(DIFF_HEADS):
        sl = slice(base + h * LANES, base + (h + 1) * LANES)
        qa_ref[0, MLA_HEADS + h, rows] = rope_b(proj[:, sl], scale_diff).astype(BF16)
        sl = slice(base + width + h * LANES, base + width + (h + 1) * LANES)
        ka_ref[0, MLA_HEADS + h, rows] = rope_b(proj[:, sl], 1.0).astype(BF16)
        sl = slice(base + 2 * width + h * LANES, base + 2 * width + (h + 1) * LANES)
        va_ref[0, MLA_HEADS // 2 + h, :, rows] = proj[:, sl].T.astype(BF16)


def _run_proj(x, pos_f, freq, g_attn, w_in_r, gq, w_uq_r, gkv, w_ukv_r):
    B, S, _ = x.shape
    tm = TM_PROJ
    full = lambda shape: pl.BlockSpec(shape, lambda b, i: (0,) * len(shape))
    head_out = lambda n: pl.BlockSpec((1, n, tm, LANES), lambda b, i: (b, 0, i, 0))
    return pl.pallas_call(
        _proj_kernel,
        grid=(B, S // tm),
        in_specs=[
            pl.BlockSpec((1, tm, D_MODEL), lambda b, i: (b, i, 0)),
            pl.BlockSpec((1, tm, 1), lambda b, i: (b, i, 0)),
            full((1, LANES)),
            full((1, D_MODEL)),
            full((D_MODEL, PROJ_COLS)),
            full((1, MLA_Q_RANK)),
            full((MLA_Q_RANK, MLA_HEADS * LANES)),
            full((1, MLA_KV_RANK)),
            full((MLA_KV_RANK, MLA_HEADS * LANES + MLA_HEADS * MLA_V_DIM)),
        ],
        out_specs=[head_out(N_QK_BLOCKS), head_out(N_QK_BLOCKS),
                   pl.BlockSpec((1, N_V_BLOCKS, LANES, tm), lambda b, i: (b, 0, 0, i))],
        out_shape=[
            jax.ShapeDtypeStruct((B, N_QK_BLOCKS, S, LANES), BF16),
            jax.ShapeDtypeStruct((B, N_QK_BLOCKS, S, LANES), BF16),
            jax.ShapeDtypeStruct((B, N_V_BLOCKS, LANES, S), BF16),
        ],
        compiler_params=pltpu.CompilerParams(
            dimension_semantics=("arbitrary", "arbitrary"), vmem_limit_bytes=VMEM_LIMIT),
        name="proj",
    )(x, pos_f, freq, g_attn, w_in_r, gq, w_uq_r, gkv, w_ukv_r)


def _attn_kernel(qa_ref, ka_ref, vt_ref, lam_ref, subln_ref, o_ref, m_ref, l_ref, acc_ref):
    qi = pl.program_id(1)
    m_ref[...] = jnp.full(m_ref.shape, NEG_BIG, F32)
    l_ref[...] = jnp.zeros(l_ref.shape, F32)
    acc_ref[...] = jnp.zeros(acc_ref.shape, F32)

    low_half = lax.broadcasted_iota(jnp.int32, (TQ, LANES), 1) < 64
    is_mla = lambda u: u < MLA_HEADS // 2
    v_rows = lambda u, s: slice(s * MLA_V_DIM, (s + 1) * MLA_V_DIM) if is_mla(u) else slice(0, DIFF_V_DIM)

    def unit_operands(u):
        if is_mla(u):
            return (qa_ref[0, 2 * u], 2 * u), (qa_ref[0, 2 * u + 1], 2 * u + 1), u
        h = u - MLA_HEADS // 2
        q = qa_ref[0, MLA_HEADS + h]
        zero = jnp.zeros_like(q)
        return ((jnp.where(low_half, q, zero), MLA_HEADS + h),
                (jnp.where(low_half, zero, q), MLA_HEADS + h), MLA_HEADS // 2 + h)

    def scores(q, k, mask):
        st = lax.dot_general(k, q, (((1,), (1,)), ((), ())), preferred_element_type=F32)
        return st if mask is None else jnp.where(mask, st, NEG_BIG)

    def softmax_update(slot, rows, st, vt):
        m_prev = m_ref[slot]
        m_new = jnp.maximum(m_prev, jnp.max(st, axis=0, keepdims=True))
        alpha = jnp.exp2(m_prev - m_new)
        pt = jnp.exp2(st - m_new)
        l_ref[slot] = alpha * l_ref[slot] + jnp.sum(pt, axis=0, keepdims=True)
        acc_ref[slot, rows] = alpha * acc_ref[slot, rows] + jnp.dot(vt[rows], pt.astype(BF16),
                                                                    preferred_element_type=F32)
        m_ref[slot] = m_new

    streams = []
    for u in range(N_UNITS):
        (q0, k0), (q1, k1), vi = unit_operands(u)
        streams += [(2 * u, v_rows(u, 0), q0, k0, vi), (2 * u + 1, v_rows(u, 1), q1, k1, vi)]

    def kv_step(j, mask):
        keys = pl.ds(pl.multiple_of(j * TK, TK), TK)
        pending = {}
        for n in range(len(streams) + SCORE_LEAD):
            if n < len(streams):
                _, _, q, kidx, _ = streams[n]
                pending[n] = scores(q, ka_ref[0, kidx, keys, :], mask)
            if n >= SCORE_LEAD:
                slot, rows, _, _, vi = streams[n - SCORE_LEAD]
                softmax_update(slot, rows, pending.pop(n - SCORE_LEAD), vt_ref[0, vi, :, keys])

    def body(j, carry):
        kv_step(j, None)
        return carry

    lax.fori_loop(0, qi, body, 0)
    key_pos = lax.broadcasted_iota(jnp.int32, (TK, TQ), 0)
    query_pos = lax.broadcasted_iota(jnp.int32, (TK, TQ), 1)
    kv_step(qi, key_pos <= query_pos)

    lam_v = lam_ref[...]
    lam = (jnp.exp(jnp.sum(lam_v[0:1] * lam_v[1:2], axis=1, keepdims=True))
           - jnp.exp(jnp.sum(lam_v[2:3] * lam_v[3:4], axis=1, keepdims=True)) + LAM_INIT)
    for u in range(N_UNITS):
        r0, r1 = v_rows(u, 0), v_rows(u, 1)
        o0 = acc_ref[2 * u, r0] / l_ref[2 * u]
        o1 = acc_ref[2 * u + 1, r1] / l_ref[2 * u + 1]
        if is_mla(u):
            out = jnp.concatenate([o0, o1], axis=0).T
            col0 = u * LANES
        else:
            d = o0 - lam * o1
            d = d * lax.rsqrt(jnp.mean(d * d, axis=0, keepdims=True) + EPS)
            out = d.T * subln_ref[...] * (1.0 - LAM_INIT)
            col0 = MLA_HEADS * MLA_V_DIM + (u - MLA_HEADS // 2) * LANES
        o_ref[0, :, col0:col0 + LANES] = out.astype(BF16)


def _run_attn(qa, ka, vt, lam_vecs, subln):
    B, _, S, _ = qa.shape
    n_slots = 2 * N_UNITS
    return pl.pallas_call(
        _attn_kernel,
        grid=(B, S // TQ),
        in_specs=[
            pl.BlockSpec((1, N_QK_BLOCKS, TQ, LANES), lambda b, i: (b, 0, i, 0)),
            pl.BlockSpec((1, N_QK_BLOCKS, S, LANES), lambda b, i: (b, 0, 0, 0)),
            pl.BlockSpec((1, N_V_BLOCKS, LANES, S), lambda b, i: (b, 0, 0, 0)),
            pl.BlockSpec((4, DIFF_HALF), lambda b, i: (0, 0)),
            pl.BlockSpec((1, DIFF_V_DIM), lambda b, i: (0, 0)),
        ],
        out_specs=pl.BlockSpec((1, TQ, D_MODEL), lambda b, i: (b, i, 0)),
        out_shape=jax.ShapeDtypeStruct((B, S, D_MODEL), BF16),
        scratch_shapes=[
            pltpu.VMEM((n_slots, 1, TQ), F32),
            pltpu.VMEM((n_slots, 1, TQ), F32),
            pltpu.VMEM((n_slots, LANES, TQ), F32),
        ],
        compiler_params=pltpu.CompilerParams(
            dimension_semantics=("arbitrary", "arbitrary"), vmem_limit_bytes=VMEM_LIMIT),
        name="attn",
    )(qa, ka, vt, lam_vecs, subln)


def _post_kernel(o_ref, x_ref, w_o_ref, g_ffn_ref, w_r_ref, b_r_ref, tri_ref,
                 h_ref, tp_ref, route_ref, cnt_ref):
    tm = x_ref.shape[0]

    @pl.when(pl.program_id(0) == 0)
    def _():
        cnt_ref[...] = jnp.zeros(cnt_ref.shape, F32)

    h = x_ref[...] + jnp.dot(o_ref[...], w_o_ref[...], preferred_element_type=F32)
    h_ref[...] = h
    t = _rms(h, g_ffn_ref[...])
    lg = jnp.dot(t.astype(BF16), w_r_ref[...], preferred_element_type=F32) + b_r_ref[...]

    lane = lax.broadcasted_iota(jnp.int32, lg.shape, 1)
    lane_f = lane.astype(F32)
    first_lane = lambda hit: jnp.min(jnp.where(hit, lane_f, float(LANES)), axis=1, keepdims=True)

    is_group = lane < N_GROUPS
    gl = jnp.where(is_group, lg, NEG_BIG)
    gmax = jnp.max(gl, axis=1, keepdims=True)
    gidx = first_lane(gl == gmax)

    lo = N_GROUPS + EXPERTS_PER_GROUP * gidx
    el = jnp.where((lane_f >= lo) & (lane_f < lo + EXPERTS_PER_GROUP), lg, NEG_BIG)
    m1 = jnp.max(el, axis=1, keepdims=True)
    i1 = first_lane(el == m1)
    el2 = jnp.where(lane_f == i1, NEG_BIG, el)
    m2 = jnp.max(el2, axis=1, keepdims=True)
    i2 = first_lane(el2 == m2)

    a = jnp.minimum(i1, i2) - lo
    b = jnp.maximum(i1, i2) - lo
    pair = a * (2 * EXPERTS_PER_GROUP - 1 - a) * 0.5 + (b - a - 1.0)
    cls = gidx * PAIRS_PER_GROUP + pair

    hit = lane_f == cls
    onehot = jnp.where(hit, 1.0, 0.0)
    before = jnp.dot(tri_ref[...], onehot.astype(BF16), preferred_element_type=F32) + cnt_ref[...]
    rank = jnp.sum(jnp.where(hit, before, 0.0), axis=1, keepdims=True)
    cnt_ref[...] += jnp.sum(onehot, axis=0, keepdims=True)

    out = jnp.zeros(lg.shape, F32)
    for k, val in enumerate((cls, rank)):
        out = jnp.where(lane == k, val, out)
    route_ref[...] = out

    for c in range(SUBLANES):
        tp_ref[pl.ds(c, tm, stride=SUBLANES), :] = t[:, c * LANES:(c + 1) * LANES]


def _run_post(o2, x2, w_o, g_ffn, w_r, b_r, tri):
    T = x2.shape[0]
    tm = TM_POST
    full = lambda shape: pl.BlockSpec(shape, lambda i: (0,) * len(shape))
    rows = lambda w: pl.BlockSpec((tm, w), lambda i: (i, 0))
    return pl.pallas_call(
        _post_kernel,
        grid=(T // tm,),
        in_specs=[rows(D_MODEL), rows(D_MODEL), full((D_MODEL, D_MODEL)), full((1, D_MODEL)),
                  full((D_MODEL, LANES)), full((1, LANES)), full((tm, tm))],
        out_specs=[rows(D_MODEL), pl.BlockSpec((tm * SUBLANES, LANES), lambda i: (i, 0)), rows(LANES),
                   full((1, LANES))],
        out_shape=[
            jax.ShapeDtypeStruct((T, D_MODEL), F32),
            jax.ShapeDtypeStruct((T * SUBLANES, LANES), F32),
            jax.ShapeDtypeStruct((T, LANES), F32),
            jax.ShapeDtypeStruct((1, LANES), F32),
        ],
        compiler_params=pltpu.CompilerParams(
            dimension_semantics=("arbitrary",), vmem_limit_bytes=VMEM_LIMIT),
        name="post",
    )(o2, x2, w_o, g_ffn, w_r, b_r, tri)


TILE_ROWS = TM_MOE * SUBLANES


TILES_PER_STEP = 4
GATHER_AHEAD = 2
GATHER_SLOTS = GATHER_AHEAD + 1


def _sorted_row(key_ref, off_ref, t):
    n_tok = key_ref.shape[0]
    key = key_ref[t]
    return off_ref[key >> (n_tok.bit_length() - 1)] + (key & (n_tok - 1))


def _moe_kernel(key_ref, off_ref, ta_ref, tb_ref, tp_hbm, w_r_ref, b_r_ref, wg_ref, wu_ref, wd_ref,
                y_ref, src_ref, xbuf, sem):
    step = pl.program_id(0)
    n_tok = key_ref.shape[0]
    n_used = off_ref[N_CLASSES] // TM_MOE

    def start_gather(tile):
        slot = tile % GATHER_SLOTS
        for r in range(TM_MOE):
            tok = src_ref[tile * TM_MOE + r]
            pltpu.make_async_copy(
                tp_hbm.at[pl.ds(pl.multiple_of(tok * SUBLANES, SUBLANES), SUBLANES), :],
                xbuf.at[pl.ds(pl.multiple_of(slot * TILE_ROWS + r * SUBLANES, SUBLANES), SUBLANES), :],
                sem.at[slot]).start(priority=r % 2)

    def wait_gather(tile):
        slot = tile % GATHER_SLOTS
        pltpu.make_async_copy(
            tp_hbm.at[pl.ds(0, TILE_ROWS), :],
            xbuf.at[pl.ds(pl.multiple_of(slot * TILE_ROWS, SUBLANES), TILE_ROWS), :],
            sem.at[slot]).wait()

    @pl.when(step == 0)
    def _():
        def fill(tile, c):
            for r in range(TM_MOE):
                src_ref[tile * TM_MOE + r] = (tile * TM_MOE + r) & (n_tok - 1)
            return c

        def invert(t, c):
            src_ref[_sorted_row(key_ref, off_ref, t)] = t
            return c

        lax.fori_loop(0, n_used + GATHER_AHEAD, fill, 0)
        lax.fori_loop(0, n_tok, invert, 0, unroll=16)
        for tile in range(GATHER_AHEAD):
            start_gather(tile)

    for sub in range(TILES_PER_STEP):
        _moe_tile(step * TILES_PER_STEP + sub, sub, n_used, ta_ref, tb_ref, w_r_ref, b_r_ref,
                  wg_ref, wu_ref, wd_ref, y_ref, xbuf, start_gather, wait_gather)


def _moe_tile(i, sub, n_used, ta_ref, tb_ref, w_r_ref, b_r_ref, wg_ref, wu_ref, wd_ref, y_ref, xbuf,
              start_gather, wait_gather):
    out0 = sub * TILE_ROWS

    @pl.when(i < n_used)
    def _():
        wait_gather(i)
        base = (i % GATHER_SLOTS) * TILE_ROWS
        x = jnp.concatenate([xbuf[pl.ds(base + c, TM_MOE, stride=SUBLANES), :] for c in range(SUBLANES)],
                            axis=1).astype(BF16)

        lg = jnp.dot(x, w_r_ref[...], preferred_element_type=F32) + b_r_ref[...]
        lane = lax.broadcasted_iota(jnp.int32, lg.shape, 1)
        pick = lambda idx: jnp.sum(jnp.where(lane == idx, lg, 0.0), axis=1, keepdims=True)
        ea, eb = ta_ref[i], tb_ref[i]
        g_logit = pick(ea // EXPERTS_PER_GROUP)
        p_group = 1.0 / jnp.sum(jnp.where(lane < N_GROUPS, jnp.exp(lg - g_logit), 0.0), axis=1, keepdims=True)
        l_a, l_b = pick(N_GROUPS + ea), pick(N_GROUPS + eb)
        gate_a = p_group / (1.0 + jnp.exp(l_b - l_a))
        gate_b = p_group / (1.0 + jnp.exp(l_a - l_b))

        def hidden(e_loc, gate):
            a = jnp.dot(x, wg_ref[e_loc], preferred_element_type=F32)
            u = jnp.dot(x, wu_ref[e_loc], preferred_element_type=F32)
            return ((a * jax.nn.sigmoid(a)) * u * gate).astype(BF16)

        a_loc, b_loc = ea % EXPERTS_PER_GROUP, eb % EXPERTS_PER_GROUP
        y = (jnp.dot(hidden(a_loc, gate_a), wd_ref[a_loc], preferred_element_type=F32)
             + jnp.dot(hidden(b_loc, gate_b), wd_ref[b_loc], preferred_element_type=F32))
        for c in range(SUBLANES):
            y_ref[pl.ds(out0 + c, TM_MOE, stride=SUBLANES), :] = y[:, c * LANES:(c + 1) * LANES]
        start_gather(i + GATHER_AHEAD)

    @pl.when(i >= n_used)
    def _():
        y_ref[pl.ds(out0, TILE_ROWS), :] = jnp.zeros((TILE_ROWS, LANES), F32)

        @pl.when(i < n_used + GATHER_AHEAD)
        def _():
            wait_gather(i)


def _run_moe(key, row_off, tile_a, tile_b, tp8, w_r, b_r, wg, wu, wd):
    n_tiles = tile_a.shape[0]
    n_tok = key.shape[0]
    assert n_tok & (n_tok - 1) == 0, "token count must be a power of two (class/rank key packing)"
    assert n_tiles % TILES_PER_STEP == 0
    group = lambda i, k, o, ta, tb: (ta[i * TILES_PER_STEP] // EXPERTS_PER_GROUP, 0, 0)
    grid_spec = pltpu.PrefetchScalarGridSpec(
        num_scalar_prefetch=4,
        grid=(n_tiles // TILES_PER_STEP,),
        in_specs=[
            pl.BlockSpec(memory_space=pl.ANY),
            pl.BlockSpec((D_MODEL, LANES), lambda i, k, o, ta, tb: (0, 0)),
            pl.BlockSpec((1, LANES), lambda i, k, o, ta, tb: (0, 0)),
            pl.BlockSpec((EXPERTS_PER_GROUP, D_MODEL, D_FF_EXPERT), group),
            pl.BlockSpec((EXPERTS_PER_GROUP, D_MODEL, D_FF_EXPERT), group),
            pl.BlockSpec((EXPERTS_PER_GROUP, D_FF_EXPERT, D_MODEL), group),
        ],
        out_specs=pl.BlockSpec((TILES_PER_STEP * TILE_ROWS, LANES), lambda i, k, o, ta, tb: (i, 0)),
        scratch_shapes=[
            pltpu.SMEM((n_tiles * TM_MOE,), jnp.int32),
            pltpu.VMEM((GATHER_SLOTS * TILE_ROWS, LANES), F32),
            pltpu.SemaphoreType.DMA((GATHER_SLOTS,)),
        ],
    )
    return pl.pallas_call(
        _moe_kernel,
        grid_spec=grid_spec,
        out_shape=jax.ShapeDtypeStruct((n_tiles * TILE_ROWS, LANES), F32),
        compiler_params=pltpu.CompilerParams(
            dimension_semantics=("arbitrary",), vmem_limit_bytes=VMEM_LIMIT),
        name="moe",
    )(key, row_off, tile_a, tile_b, tp8, w_r, b_r, wg, wu, wd)


FIN_ROWS = TM_FIN * SUBLANES


def _final_kernel(key_ref, off_ref, h_ref, y_hbm, g_ref, out_ref, ybuf, sem):
    i = pl.program_id(0)
    slot = i % 2

    def start_gather(tile, slot):
        def body(r2, c):
            for q in range(2):
                r = 2 * r2 + q
                p = _sorted_row(key_ref, off_ref, tile * TM_FIN + r)
                pltpu.make_async_copy(
                    y_hbm.at[pl.ds(pl.multiple_of(p * SUBLANES, SUBLANES), SUBLANES), :],
                    ybuf.at[pl.ds(pl.multiple_of(slot * FIN_ROWS + r * SUBLANES, SUBLANES), SUBLANES), :],
                    sem.at[slot]).start(priority=q)
            return c
        lax.fori_loop(0, TM_FIN // 2, body, 0, unroll=4)

    @pl.when(i == 0)
    def _():
        start_gather(0, 0)

    @pl.when(i + 1 < pl.num_programs(0))
    def _():
        start_gather(i + 1, 1 - slot)

    pltpu.make_async_copy(
        y_hbm.at[pl.ds(0, FIN_ROWS), :],
        ybuf.at[pl.ds(pl.multiple_of(slot * FIN_ROWS, SUBLANES), FIN_ROWS), :],
        sem.at[slot]).wait()
    base = slot * FIN_ROWS
    y = jnp.concatenate([ybuf[pl.ds(base + c, TM_FIN, stride=SUBLANES), :] for c in range(SUBLANES)], axis=1)
    out_ref[...] = _rms(h_ref[...] + y, g_ref[...])


def _run_final(key, row_off, h2, ys8, g_final):
    T = h2.shape[0]
    grid_spec = pltpu.PrefetchScalarGridSpec(
        num_scalar_prefetch=2,
        grid=(T // TM_FIN,),
        in_specs=[
            pl.BlockSpec((TM_FIN, D_MODEL), lambda i, k, o: (i, 0)),
            pl.BlockSpec(memory_space=pl.ANY),
            pl.BlockSpec((1, D_MODEL), lambda i, k, o: (0, 0)),
        ],
        out_specs=pl.BlockSpec((TM_FIN, D_MODEL), lambda i, k, o: (i, 0)),
        scratch_shapes=[pltpu.VMEM((2 * FIN_ROWS, LANES), F32), pltpu.SemaphoreType.DMA((2,))],
    )
    return pl.pallas_call(
        _final_kernel,
        grid_spec=grid_spec,
        out_shape=jax.ShapeDtypeStruct((T, D_MODEL), F32),
        compiler_params=pltpu.CompilerParams(
            dimension_semantics=("arbitrary",), vmem_limit_bytes=VMEM_LIMIT),
        name="final",
    )(key, row_off, h2, ys8, g_final)


def _prep_attention_weights(w_in, w_uq, w_ukv):
    d = w_in.shape[0]
    c0 = MLA_Q_RANK + MLA_KV_RANK
    kpe = w_in[:, c0:c0 + MLA_QK_ROPE]
    kpe_blk = jnp.concatenate(
        [jnp.zeros((d, MLA_QK_NOPE), F32), kpe, jnp.zeros((d, LANES - MLA_QK_NOPE - MLA_QK_ROPE), F32)], axis=1)
    w_in_r = jnp.concatenate([w_in[:, :c0], kpe_blk, w_in[:, c0 + MLA_QK_ROPE:]], axis=1).astype(BF16)

    dqk = MLA_QK_NOPE + MLA_QK_ROPE
    uq = w_uq.reshape(MLA_Q_RANK, MLA_HEADS, dqk)
    uq = jnp.pad(uq, ((0, 0), (0, 0), (0, LANES - dqk))).reshape(MLA_Q_RANK, MLA_HEADS * LANES).astype(BF16)

    ukv = w_ukv.reshape(MLA_KV_RANK, MLA_HEADS, MLA_QK_NOPE + MLA_V_DIM)
    uk = jnp.pad(ukv[:, :, :MLA_QK_NOPE], ((0, 0), (0, 0), (0, LANES - MLA_QK_NOPE)))
    uk = uk.reshape(MLA_KV_RANK, MLA_HEADS * LANES)
    uv = ukv[:, :, MLA_QK_NOPE:].reshape(MLA_KV_RANK, MLA_HEADS * MLA_V_DIM)
    w_ukv_r = jnp.concatenate([uk, uv], axis=1).astype(BF16)
    return w_in_r, uq, w_ukv_r


def _rope_freq_lanes():
    fb = ROPE_THETA ** (-jnp.arange(DIFF_HALF // 2, dtype=F32) / (DIFF_HALF // 2))
    fa = ROPE_THETA ** (-jnp.arange(MLA_QK_ROPE // 2, dtype=F32) / (MLA_QK_ROPE // 2))
    return jnp.concatenate([fb, fb, fa, fa, jnp.zeros((LANES - 96,), F32)])[None, :]


def kernel(x, positions, attn_norm_g, w_in, q_norm_g, w_uq, kv_norm_g, w_ukv, lambda_q1, lambda_k1, lambda_q2, lambda_k2, subln_g, w_o, ffn_norm_g, w_router_group, b_router_group, w_router_expert, b_router_expert, w_gate, w_up, w_down, final_norm_g):
    B, S, D = x.shape
    T = B * S
    l = 0

    w_in_r, w_uq_r, w_ukv_r = _prep_attention_weights(w_in[l], w_uq[l], w_ukv[l])
    pos_f = positions.astype(F32)[:, :, None]
    qa, ka, va = _run_proj(x, pos_f, _rope_freq_lanes(), attn_norm_g[l][None, :], w_in_r,
                           q_norm_g[l][None, :], w_uq_r, kv_norm_g[l][None, :], w_ukv_r)

    lam_vecs = jnp.stack([lambda_q1[l], lambda_k1[l], lambda_q2[l], lambda_k2[l]]).astype(F32)
    o = _run_attn(qa, ka, va, lam_vecs, subln_g[l][None, :])

    w_r = jnp.concatenate(
        [w_router_group[l], w_router_expert[l].reshape(D, N_EXPERTS),
         jnp.zeros((D, LANES - N_GROUPS - N_EXPERTS), F32)], axis=1).astype(BF16)
    b_r = jnp.concatenate(
        [b_router_group[l], b_router_expert[l].reshape(N_EXPERTS),
         jnp.zeros((LANES - N_GROUPS - N_EXPERTS,), F32)])[None, :].astype(F32)
    ii = jnp.arange(TM_POST)
    tri = (ii[None, :] < ii[:, None]).astype(BF16)
    x2 = x.reshape(T, D)
    h2, tp8, route, counts = _run_post(o.reshape(T, D), x2, w_o[l].astype(BF16), ffn_norm_g[l][None, :],
                                       w_r, b_r, tri)

    pad_tiles = N_GROUPS * (TILES_PER_STEP - 1)
    n_tiles_max = T // TM_MOE + N_CLASSES - 1 + pad_tiles + GATHER_AHEAD
    n_tiles_max += -n_tiles_max % TILES_PER_STEP
    key = route[:, 0].astype(jnp.int32) * T + route[:, 1].astype(jnp.int32)
    cnt = counts[0, :N_CLASSES].astype(jnp.int32)
    tiles_gc = ((cnt + TM_MOE - 1) // TM_MOE).reshape(N_GROUPS, PAIRS_PER_GROUP)
    group_pad = -jnp.sum(tiles_gc, axis=1) % TILES_PER_STEP
    tiles_c = tiles_gc.at[:, -1].add(group_pad).reshape(N_CLASSES)
    tile_end = jnp.cumsum(tiles_c)
    n_tiles = tile_end[-1]
    row_off = jnp.concatenate([(tile_end - tiles_c) * TM_MOE, (n_tiles * TM_MOE)[None],
                               jnp.zeros((LANES - N_CLASSES - 1,), jnp.int32)])
    tile_ids = jnp.arange(n_tiles_max, dtype=jnp.int32)
    tile_cls = jnp.sum(jnp.minimum(tile_ids, n_tiles - 1)[:, None] >= tile_end[None, :], axis=1, dtype=jnp.int32)
    pair_a, pair_b = [], []
    for g in range(N_GROUPS):
        for a in range(EXPERTS_PER_GROUP):
            for b in range(a + 1, EXPERTS_PER_GROUP):
                pair_a.append(g * EXPERTS_PER_GROUP + a)
                pair_b.append(g * EXPERTS_PER_GROUP + b)
    cls_onehot = (tile_cls[:, None] == jnp.arange(N_CLASSES, dtype=jnp.int32)[None, :]).astype(jnp.int32)
    tile_a = cls_onehot @ jnp.array(pair_a, jnp.int32)
    tile_b = cls_onehot @ jnp.array(pair_b, jnp.int32)

    ys8 = _run_moe(key, row_off, tile_a, tile_b, tp8, w_r, b_r,
                   w_gate[l].astype(BF16), w_up[l].astype(BF16), w_down[l].astype(BF16))
    out = _run_final(key, row_off, h2, ys8, final_norm_g[None, :])
    return out.reshape(B, S, D)
```

```python
import functools
import math

import jax
import jax.numpy as jnp
from jax import lax
from jax.experimental import pallas as pl
from jax.experimental.pallas import tpu as pltpu

F32 = jnp.float32
BF16 = jnp.bfloat16

D_MODEL = 1024
MLA_HEADS = 8
MLA_QK_NOPE = 64
MLA_QK_ROPE = 32
MLA_V_DIM = 64
MLA_Q_RANK = 384
MLA_KV_RANK = 256
DIFF_HEADS = 4
DIFF_HALF = 64
DIFF_V_DIM = 2 * DIFF_HALF
N_GROUPS = 4
EXPERTS_PER_GROUP = 8
N_EXPERTS = N_GROUPS * EXPERTS_PER_GROUP
D_FF_EXPERT = 256
ROPE_THETA = 10000.0
EPS = 1e-6
LAM_INIT = 0.8 - 0.6 * math.exp(-0.3 * 0)

LANES = 128
N_QK_BLOCKS = MLA_HEADS + DIFF_HEADS
N_V_BLOCKS = MLA_HEADS // 2 + DIFF_HEADS
N_UNITS = MLA_HEADS // 2 + DIFF_HEADS
PROJ_COLS = MLA_Q_RANK + MLA_KV_RANK + LANES + 3 * DIFF_HEADS * DIFF_V_DIM
LOG2E = 1.4426950408889634
NEG_BIG = -1e30

TM_PROJ = 512
PROJ_SPLIT = 2
TQ = 256
TK = 256
SCORE_LEAD = 6
KV_UNROLL = 4
TM_POST = 512
TM_MOE = 128
TM_FIN = 256
SUBLANES = 8
PAIRS_PER_GROUP = EXPERTS_PER_GROUP * (EXPERTS_PER_GROUP - 1) // 2
N_CLASSES = N_GROUPS * PAIRS_PER_GROUP
VMEM_LIMIT = 48 * 1024 * 1024


def _rms(x, g):
    return x * lax.rsqrt(jnp.mean(x * x, axis=-1, keepdims=True) + EPS) * g


def _proj_kernel(x_ref, pos_ref, freq_ref, g_attn_ref, w_in_ref, gq_ref, w_uq_ref, gkv_ref, w_ukv_ref,
                 qa_ref, ka_ref, va_ref):
    tm = x_ref.shape[1] // PROJ_SPLIT
    for part in range(PROJ_SPLIT):
        _proj_rows(slice(part * tm, (part + 1) * tm), x_ref, pos_ref, freq_ref, g_attn_ref, w_in_ref, gq_ref,
                   w_uq_ref, gkv_ref, w_ukv_ref, qa_ref, ka_ref, va_ref)


def _proj_rows(rows, x_ref, pos_ref, freq_ref, g_attn_ref, w_in_ref, gq_ref, w_uq_ref, gkv_ref, w_ukv_ref,
               qa_ref, ka_ref, va_ref):
    xn = _rms(x_ref[0, rows], g_attn_ref[...]).astype(BF16)
    proj = jnp.dot(xn, w_in_ref[...], preferred_element_type=F32)

    ang = pos_ref[0, rows] * freq_ref[...]
    cos_t, sin_t = jnp.cos(ang), jnp.sin(ang)
    lane = lax.broadcasted_iota(jnp.int32, ang.shape, 1)
    in_rope = (lane >= 64) & (lane < 96)
    cos_a = jnp.where(in_rope, cos_t, 1.0)
    sin_a_hi = jnp.where((lane >= 80) & (lane < 96), sin_t, 0.0)
    sin_a_lo = jnp.where((lane >= 64) & (lane < 80), -sin_t, 0.0)
    cos_b = jnp.where(lane < 64, cos_t, pltpu.roll(cos_t, 64, 1))
    sin_b = jnp.where(lane < 64, sin_t, pltpu.roll(sin_t, 64, 1))
    sub = lane % 64
    sin_b_hi = jnp.where(sub >= 32, sin_b, 0.0)
    sin_b_lo = jnp.where(sub < 32, -sin_b, 0.0)

    def rope_a(xb, scale):
        return (xb * (cos_a * scale) + pltpu.roll(xb, 16, 1) * (sin_a_hi * scale)
                + pltpu.roll(xb, 112, 1) * (sin_a_lo * scale))

    def rope_b(xb, scale):
        return (xb * (cos_b * scale) + pltpu.roll(xb, 32, 1) * (sin_b_hi * scale)
                + pltpu.roll(xb, 96, 1) * (sin_b_lo * scale))

    c_q = proj[:, :MLA_Q_RANK]
    c_kv = proj[:, MLA_Q_RANK:MLA_Q_RANK + MLA_KV_RANK]
    base = MLA_Q_RANK + MLA_KV_RANK
    kpe = rope_a(proj[:, base:base + LANES], 1.0)
    base += LANES

    q = jnp.dot(_rms(c_q, gq_ref[...]).astype(BF16), w_uq_ref[...], preferred_element_type=F32)
    kv = jnp.dot(_rms(c_kv, gkv_ref[...]).astype(BF16), w_ukv_ref[...], preferred_element_type=F32)

    scale_mla = (MLA_QK_NOPE + MLA_QK_ROPE) ** -0.5 * LOG2E
    scale_diff = DIFF_HALF ** -0.5 * LOG2E
    for h in range(MLA_HEADS):
        sl = slice(h * LANES, (h + 1) * LANES)
        qa_ref[0, h, rows] = rope_a(q[:, sl], scale_mla).astype(BF16)
        ka_ref[0, h, rows] = (kv[:, sl] + kpe).astype(BF16)
    vbase = MLA_HEADS * LANES
    for hp in range(MLA_HEADS // 2):
        va_ref[0, hp, :, rows] = kv[:, vbase + hp * LANES: vbase + (hp + 1) * LANES].T.astype(BF16)
    width = DIFF_HEADS * DIFF_V_DIM
    for h in range(DIFF_HEADS):
        sl = slice(base + h * LANES, base + (h + 1) * LANES)
        qa_ref[0, MLA_HEADS + h, rows] = rope_b(proj[:, sl], scale_diff).astype(BF16)
        sl = slice(base + width + h * LANES, base + width + (h + 1) * LANES)
        ka_ref[0, MLA_HEADS + h, rows] = rope_b(proj[:, sl], 1.0).astype(BF16)
        sl = slice(base + 2 * width + h * LANES, base + 2 * width + (h + 1) * LANES)
        va_ref[0, MLA_HEADS // 2 + h, :, rows] = proj[:, sl].T.astype(BF16)


def _run_proj(x, pos_f, freq, g_attn, w_in_r, gq, w_uq_r, gkv, w_ukv_r):
    B, S, _ = x.shape
    tm = TM_PROJ
    full = lambda shape: pl.BlockSpec(shape, lambda b, i: (0,) * len(shape))
    head_out = lambda n: pl.BlockSpec((1, n, tm, LANES), lambda b, i: (b, 0, i, 0))
    return pl.pallas_call(
        _proj_kernel,
        grid=(B, S // tm),
        in_specs=[
            pl.BlockSpec((1, tm, D_MODEL), lambda b, i: (b, i, 0)),
            pl.BlockSpec((1, tm, 1), lambda b, i: (b, i, 0)),
            full((1, LANES)),
            full((1, D_MODEL)),
            full((D_MODEL, PROJ_COLS)),
            full((1, MLA_Q_RANK)),
            full((MLA_Q_RANK, MLA_HEADS * LANES)),
            full((1, MLA_KV_RANK)),
            full((MLA_KV_RANK, MLA_HEADS * LANES + MLA_HEADS * MLA_V_DIM)),
        ],
        out_specs=[head_out(N_QK_BLOCKS), head_out(N_QK_BLOCKS),
                   pl.BlockSpec((1, N_V_BLOCKS, LANES, tm), lambda b, i: (b, 0, 0, i))],
        out_shape=[
            jax.ShapeDtypeStruct((B, N_QK_BLOCKS, S, LANES), BF16),
            jax.ShapeDtypeStruct((B, N_QK_BLOCKS, S, LANES), BF16),
            jax.ShapeDtypeStruct((B, N_V_BLOCKS, LANES, S), BF16),
        ],
        compiler_params=pltpu.CompilerParams(
            dimension_semantics=("arbitrary", "arbitrary"), vmem_limit_bytes=VMEM_LIMIT),
        name="proj",
    )(x, pos_f, freq, g_attn, w_in_r, gq, w_uq_r, gkv, w_ukv_r)


def _attn_kernel(qa_ref, ka_ref, vt_ref, lam_ref, subln_ref, o_ref, m_ref, l_ref, acc_ref):
    qi = pl.program_id(1)
    m_ref[...] = jnp.full(m_ref.shape, NEG_BIG, F32)
    l_ref[...] = jnp.zeros(l_ref.shape, F32)
    acc_ref[...] = jnp.zeros(acc_ref.shape, F32)

    low_half = lax.broadcasted_iota(jnp.int32, (TQ, LANES), 1) < 64
    is_mla = lambda u: u < MLA_HEADS // 2
    v_rows = lambda u, s: slice(s * MLA_V_DIM, (s + 1) * MLA_V_DIM) if is_mla(u) else slice(0, DIFF_V_DIM)

    def unit_operands(u):
        if is_mla(u):
            return (qa_ref[0, 2 * u], 2 * u), (qa_ref[0, 2 * u + 1], 2 * u + 1), u
        h = u - MLA_HEADS // 2
        q = qa_ref[0, MLA_HEADS + h]
        zero = jnp.zeros_like(q)
        return ((jnp.where(low_half, q, zero), MLA_HEADS + h),
                (jnp.where(low_half, zero, q), MLA_HEADS + h), MLA_HEADS // 2 + h)

    def scores(q, k, mask):
        st = lax.dot_general(k, q, (((1,), (1,)), ((), ())), preferred_element_type=F32)
        return st if mask is None else jnp.where(mask, st, NEG_BIG)

    def softmax_update(slot, rows, st, vt):
        m_prev = m_ref[slot]
        m_new = jnp.maximum(m_prev, jnp.max(st, axis=0, keepdims=True))
        alpha = jnp.exp2(m_prev - m_new)
        pt = jnp.exp2(st - m_new)
        l_ref[slot] = alpha * l_ref[slot] + jnp.sum(pt, axis=0, keepdims=True)
        acc_ref[slot, rows] = alpha * acc_ref[slot, rows] + jnp.dot(vt[rows], pt.astype(BF16),
                                                                    preferred_element_type=F32)
        m_ref[slot] = m_new

    streams = []
    for u in range(N_UNITS):
        (q0, k0), (q1, k1), vi = unit_operands(u)
        streams += [(2 * u, v_rows(u, 0), q0, k0, vi), (2 * u + 1, v_rows(u, 1), q1, k1, vi)]

    def kv_tiles(tiles):
        work = [(j, mask, st) for j, mask in tiles for st in streams]
        pending = {}
        for n in range(len(work) + SCORE_LEAD):
            if n < len(work):
                j, mask, (_, _, q, kidx, _) = work[n]
                keys = pl.ds(pl.multiple_of(j * TK, TK), TK)
                pending[n] = scores(q, ka_ref[0, kidx, keys, :], mask)
            if n >= SCORE_LEAD:
                j, _, (slot, rows, _, _, vi) = work[n - SCORE_LEAD]
                keys = pl.ds(pl.multiple_of(j * TK, TK), TK)
                softmax_update(slot, rows, pending.pop(n - SCORE_LEAD), vt_ref[0, vi, :, keys])

    def body(jj, carry):
        kv_tiles([(KV_UNROLL * jj + d, None) for d in range(KV_UNROLL)])
        return carry

    lax.fori_loop(0, qi // KV_UNROLL, body, 0)
    key_pos = lax.broadcasted_iota(jnp.int32, (TK, TQ), 0)
    query_pos = lax.broadcasted_iota(jnp.int32, (TK, TQ), 1)
    causal = key_pos <= query_pos
    for rest in range(KV_UNROLL):
        @pl.when(qi % KV_UNROLL == rest)
        def _():
            kv_tiles([(qi - rest + d, None) for d in range(rest)] + [(qi, causal)])

    lam_v = lam_ref[...]
    lam = (jnp.exp(jnp.sum(lam_v[0:1] * lam_v[1:2], axis=1, keepdims=True))
           - jnp.exp(jnp.sum(lam_v[2:3] * lam_v[3:4], axis=1, keepdims=True)) + LAM_INIT)
    for u in range(N_UNITS):
        r0, r1 = v_rows(u, 0), v_rows(u, 1)
        o0 = acc_ref[2 * u, r0] / l_ref[2 * u]
        o1 = acc_ref[2 * u + 1, r1] / l_ref[2 * u + 1]
        if is_mla(u):
            out = jnp.concatenate([o0, o1], axis=0).T
            col0 = u * LANES
        else:
            d = o0 - lam * o1
            d = d * lax.rsqrt(jnp.mean(d * d, axis=0, keepdims=True) + EPS)
            out = d.T * subln_ref[...] * (1.0 - LAM_INIT)
            col0 = MLA_HEADS * MLA_V_DIM + (u - MLA_HEADS // 2) * LANES
        o_ref[0, :, col0:col0 + LANES] = out.astype(BF16)


def _run_attn(qa, ka, vt, lam_vecs, subln):
    B, _, S, _ = qa.shape
    n_slots = 2 * N_UNITS
    return pl.pallas_call(
        _attn_kernel,
        grid=(B, S // TQ),
        in_specs=[
            pl.BlockSpec((1, N_QK_BLOCKS, TQ, LANES), lambda b, i: (b, 0, i, 0)),
            pl.BlockSpec((1, N_QK_BLOCKS, S, LANES), lambda b, i: (b, 0, 0, 0)),
            pl.BlockSpec((1, N_V_BLOCKS, LANES, S), lambda b, i: (b, 0, 0, 0)),
            pl.BlockSpec((4, DIFF_HALF), lambda b, i: (0, 0)),
            pl.BlockSpec((1, DIFF_V_DIM), lambda b, i: (0, 0)),
        ],
        out_specs=pl.BlockSpec((1, TQ, D_MODEL), lambda b, i: (b, i, 0)),
        out_shape=jax.ShapeDtypeStruct((B, S, D_MODEL), BF16),
        scratch_shapes=[
            pltpu.VMEM((n_slots, 1, TQ), F32),
            pltpu.VMEM((n_slots, 1, TQ), F32),
            pltpu.VMEM((n_slots, LANES, TQ), F32),
        ],
        compiler_params=pltpu.CompilerParams(
            dimension_semantics=("arbitrary", "arbitrary"), vmem_limit_bytes=VMEM_LIMIT),
        name="attn",
    )(qa, ka, vt, lam_vecs, subln)


def _post_kernel(o_ref, x_ref, w_o_ref, g_ffn_ref, w_r_ref, b_r_ref, tri_ref,
                 h_ref, tp_ref, route_ref, cnt_ref):
    tm = x_ref.shape[0]

    @pl.when(pl.program_id(0) == 0)
    def _():
        cnt_ref[...] = jnp.zeros(cnt_ref.shape, F32)

    h = x_ref[...] + jnp.dot(o_ref[...], w_o_ref[...], preferred_element_type=F32)
    h_ref[...] = h
    t = _rms(h, g_ffn_ref[...])
    lg = jnp.dot(t.astype(BF16), w_r_ref[...], preferred_element_type=F32) + b_r_ref[...]

    lane = lax.broadcasted_iota(jnp.int32, lg.shape, 1)
    lane_f = lane.astype(F32)
    first_lane = lambda hit: jnp.min(jnp.where(hit, lane_f, float(LANES)), axis=1, keepdims=True)

    is_group = lane < N_GROUPS
    gl = jnp.where(is_group, lg, NEG_BIG)
    gmax = jnp.max(gl, axis=1, keepdims=True)
    gidx = first_lane(gl == gmax)

    lo = N_GROUPS + EXPERTS_PER_GROUP * gidx
    el = jnp.where((lane_f >= lo) & (lane_f < lo + EXPERTS_PER_GROUP), lg, NEG_BIG)
    m1 = jnp.max(el, axis=1, keepdims=True)
    i1 = first_lane(el == m1)
    el2 = jnp.where(lane_f == i1, NEG_BIG, el)
    m2 = jnp.max(el2, axis=1, keepdims=True)
    i2 = first_lane(el2 == m2)

    a = jnp.minimum(i1, i2) - lo
    b = jnp.maximum(i1, i2) - lo
    pair = a * (2 * EXPERTS_PER_GROUP - 1 - a) * 0.5 + (b - a - 1.0)
    cls = gidx * PAIRS_PER_GROUP + pair

    hit = lane_f == cls
    onehot = jnp.where(hit, 1.0, 0.0)
    before = jnp.dot(tri_ref[...], onehot.astype(BF16), preferred_element_type=F32) + cnt_ref[...]
    rank = jnp.sum(jnp.where(hit, before, 0.0), axis=1, keepdims=True)
    cnt_ref[...] += jnp.sum(onehot, axis=0, keepdims=True)

    out = jnp.zeros(lg.shape, F32)
    for k, val in enumerate((cls, rank)):
        out = jnp.where(lane == k, val, out)
    route_ref[...] = out

    for c in range(SUBLANES):
        tp_ref[pl.ds(c, tm, stride=SUBLANES), :] = t[:, c * LANES:(c + 1) * LANES]


def _run_post(o2, x2, w_o, g_ffn, w_r, b_r, tri):
    T = x2.shape[0]
    tm = TM_POST
    full = lambda shape: pl.BlockSpec(shape, lambda i: (0,) * len(shape))
    rows = lambda w: pl.BlockSpec((tm, w), lambda i: (i, 0))
    return pl.pallas_call(
        _post_kernel,
        grid=(T // tm,),
        in_specs=[rows(D_MODEL), rows(D_MODEL), full((D_MODEL, D_MODEL)), full((1, D_MODEL)),
                  full((D_MODEL, LANES)), full((1, LANES)), full((tm, tm))],
        out_specs=[rows(D_MODEL), pl.BlockSpec((tm * SUBLANES, LANES), lambda i: (i, 0)), rows(LANES),
                   full((1, LANES))],
        out_shape=[
            jax.ShapeDtypeStruct((T, D_MODEL), F32),
            jax.ShapeDtypeStruct((T * SUBLANES, LANES), F32),
            jax.ShapeDtypeStruct((T, LANES), F32),
            jax.ShapeDtypeStruct((1, LANES), F32),
        ],
        compiler_params=pltpu.CompilerParams(
            dimension_semantics=("arbitrary",), vmem_limit_bytes=VMEM_LIMIT),
        name="post",
    )(o2, x2, w_o, g_ffn, w_r, b_r, tri)


TILE_ROWS = TM_MOE * SUBLANES


TILES_PER_STEP = 4
GATHER_AHEAD = 2
GATHER_SLOTS = GATHER_AHEAD + 1


def _sorted_row(key_ref, off_ref, t):
    n_tok = key_ref.shape[0]
    key = key_ref[t]
    return off_ref[key >> (n_tok.bit_length() - 1)] + (key & (n_tok - 1))


def _moe_kernel(key_ref, off_ref, ta_ref, tb_ref, tp_hbm, w_r_ref, b_r_ref, wg_ref, wu_ref, wd_ref,
                y_ref, src_ref, xbuf, sem):
    step = pl.program_id(0)
    n_tok = key_ref.shape[0]
    n_used = off_ref[N_CLASSES] // TM_MOE

    def start_gather(tile):
        slot = tile % GATHER_SLOTS
        for r in range(TM_MOE):
            tok = src_ref[tile * TM_MOE + r]
            pltpu.make_async_copy(
                tp_hbm.at[pl.ds(pl.multiple_of(tok * SUBLANES, SUBLANES), SUBLANES), :],
                xbuf.at[pl.ds(pl.multiple_of(slot * TILE_ROWS + r * SUBLANES, SUBLANES), SUBLANES), :],
                sem.at[slot]).start(priority=r % 2)

    def wait_gather(tile):
        slot = tile % GATHER_SLOTS
        pltpu.make_async_copy(
            tp_hbm.at[pl.ds(0, TILE_ROWS), :],
            xbuf.at[pl.ds(pl.multiple_of(slot * TILE_ROWS, SUBLANES), TILE_ROWS), :],
            sem.at[slot]).wait()

    @pl.when(step == 0)
    def _():
        def fill(tile, c):
            for r in range(TM_MOE):
                src_ref[tile * TM_MOE + r] = (tile * TM_MOE + r) & (n_tok - 1)
            return c

        def invert(t, c):
            src_ref[_sorted_row(key_ref, off_ref, t)] = t
            return c

        lax.fori_loop(0, n_used + GATHER_AHEAD, fill, 0)
        lax.fori_loop(0, n_tok, invert, 0, unroll=16)
        for tile in range(GATHER_AHEAD):
            start_gather(tile)

    for sub in range(TILES_PER_STEP):
        _moe_tile(step * TILES_PER_STEP + sub, sub, n_used, ta_ref, tb_ref, w_r_ref, b_r_ref,
                  wg_ref, wu_ref, wd_ref, y_ref, xbuf, start_gather, wait_gather)


def _moe_tile(i, sub, n_used, ta_ref, tb_ref, w_r_ref, b_r_ref, wg_ref, wu_ref, wd_ref, y_ref, xbuf,
              start_gather, wait_gather):
    out0 = sub * TILE_ROWS

    @pl.when(i < n_used)
    def _():
        wait_gather(i)
        base = (i % GATHER_SLOTS) * TILE_ROWS
        x = jnp.concatenate([xbuf[pl.ds(base + c, TM_MOE, stride=SUBLANES), :] for c in range(SUBLANES)],
                            axis=1).astype(BF16)

        lg = jnp.dot(x, w_r_ref[...], preferred_element_type=F32) + b_r_ref[...]
        lane = lax.broadcasted_iota(jnp.int32, lg.shape, 1)
        pick = lambda idx: jnp.sum(jnp.where(lane == idx, lg, 0.0), axis=1, keepdims=True)
        ea, eb = ta_ref[i], tb_ref[i]
        g_logit = pick(ea // EXPERTS_PER_GROUP)
        p_group = 1.0 / jnp.sum(jnp.where(lane < N_GROUPS, jnp.exp(lg - g_logit), 0.0), axis=1, keepdims=True)
        l_a, l_b = pick(N_GROUPS + ea), pick(N_GROUPS + eb)
        gate_a = p_group / (1.0 + jnp.exp(l_b - l_a))
        gate_b = p_group / (1.0 + jnp.exp(l_a - l_b))

        def hidden(e_loc, gate):
            a = jnp.dot(x, wg_ref[e_loc], preferred_element_type=F32)
            u = jnp.dot(x, wu_ref[e_loc], preferred_element_type=F32)
            return ((a * jax.nn.sigmoid(a)) * u * gate).astype(BF16)

        a_loc, b_loc = ea % EXPERTS_PER_GROUP, eb % EXPERTS_PER_GROUP
        y = (jnp.dot(hidden(a_loc, gate_a), wd_ref[a_loc], preferred_element_type=F32)
             + jnp.dot(hidden(b_loc, gate_b), wd_ref[b_loc], preferred_element_type=F32))
        for c in range(SUBLANES):
            y_ref[pl.ds(out0 + c, TM_MOE, stride=SUBLANES), :] = y[:, c * LANES:(c + 1) * LANES]
        start_gather(i + GATHER_AHEAD)

    @pl.when(i >= n_used)
    def _():
        y_ref[pl.ds(out0, TILE_ROWS), :] = jnp.zeros((TILE_ROWS, LANES), F32)

        @pl.when(i < n_used + GATHER_AHEAD)
        def _():
            wait_gather(i)


def _run_moe(key, row_off, tile_a, tile_b, tp8, w_r, b_r, wg, wu, wd):
    n_tiles = tile_a.shape[0]
    n_tok = key.shape[0]
    assert n_tok & (n_tok - 1) == 0, "token count must be a power of two (class/rank key packing)"
    assert n_tiles % TILES_PER_STEP == 0
    group = lambda i, k, o, ta, tb: (ta[i * TILES_PER_STEP] // EXPERTS_PER_GROUP, 0, 0)
    grid_spec = pltpu.PrefetchScalarGridSpec(
        num_scalar_prefetch=4,
        grid=(n_tiles // TILES_PER_STEP,),
        in_specs=[
            pl.BlockSpec(memory_space=pl.ANY),
            pl.BlockSpec((D_MODEL, LANES), lambda i, k, o, ta, tb: (0, 0)),
            pl.BlockSpec((1, LANES), lambda i, k, o, ta, tb: (0, 0)),
            pl.BlockSpec((EXPERTS_PER_GROUP, D_MODEL, D_FF_EXPERT), group),
            pl.BlockSpec((EXPERTS_PER_GROUP, D_MODEL, D_FF_EXPERT), group),
            pl.BlockSpec((EXPERTS_PER_GROUP, D_FF_EXPERT, D_MODEL), group),
        ],
        out_specs=pl.BlockSpec((TILES_PER_STEP * TILE_ROWS, LANES), lambda i, k, o, ta, tb: (i, 0)),
        scratch_shapes=[
            pltpu.SMEM((n_tiles * TM_MOE,), jnp.int32),
            pltpu.VMEM((GATHER_SLOTS * TILE_ROWS, LANES), F32),
            pltpu.SemaphoreType.DMA((GATHER_SLOTS,)),
        ],
    )
    return pl.pallas_call(
        _moe_kernel,
        grid_spec=grid_spec,
        out_shape=jax.ShapeDtypeStruct((n_tiles * TILE_ROWS, LANES), F32),
        compiler_params=pltpu.CompilerParams(
            dimension_semantics=("arbitrary",), vmem_limit_bytes=VMEM_LIMIT),
        name="moe",
    )(key, row_off, tile_a, tile_b, tp8, w_r, b_r, wg, wu, wd)


FIN_ROWS = TM_FIN * SUBLANES


def _final_kernel(key_ref, off_ref, h_ref, y_hbm, g_ref, out_ref, ybuf, sem):
    i = pl.program_id(0)
    slot = i % 2

    def start_gather(tile, slot):
        def body(r2, c):
            for q in range(2):
                r = 2 * r2 + q
                p = _sorted_row(key_ref, off_ref, tile * TM_FIN + r)
                pltpu.make_async_copy(
                    y_hbm.at[pl.ds(pl.multiple_of(p * SUBLANES, SUBLANES), SUBLANES), :],
                    ybuf.at[pl.ds(pl.multiple_of(slot * FIN_ROWS + r * SUBLANES, SUBLANES), SUBLANES), :],
                    sem.at[slot]).start(priority=q)
            return c
        lax.fori_loop(0, TM_FIN // 2, body, 0, unroll=4)

    @pl.when(i == 0)
    def _():
        start_gather(0, 0)

    @pl.when(i + 1 < pl.num_programs(0))
    def _():
        start_gather(i + 1, 1 - slot)

    pltpu.make_async_copy(
        y_hbm.at[pl.ds(0, FIN_ROWS), :],
        ybuf.at[pl.ds(pl.multiple_of(slot * FIN_ROWS, SUBLANES), FIN_ROWS), :],
        sem.at[slot]).wait()
    base = slot * FIN_ROWS
    y = jnp.concatenate([ybuf[pl.ds(base + c, TM_FIN, stride=SUBLANES), :] for c in range(SUBLANES)], axis=1)
    out_ref[...] = _rms(h_ref[...] + y, g_ref[...])


def _run_final(key, row_off, h2, ys8, g_final):
    T = h2.shape[0]
    grid_spec = pltpu.PrefetchScalarGridSpec(
        num_scalar_prefetch=2,
        grid=(T // TM_FIN,),
        in_specs=[
            pl.BlockSpec((TM_FIN, D_MODEL), lambda i, k, o: (i, 0)),
            pl.BlockSpec(memory_space=pl.ANY),
            pl.BlockSpec((1, D_MODEL), lambda i, k, o: (0, 0)),
        ],
        out_specs=pl.BlockSpec((TM_FIN, D_MODEL), lambda i, k, o: (i, 0)),
        scratch_shapes=[pltpu.VMEM((2 * FIN_ROWS, LANES), F32), pltpu.SemaphoreType.DMA((2,))],
    )
    return pl.pallas_call(
        _final_kernel,
        grid_spec=grid_spec,
        out_shape=jax.ShapeDtypeStruct((T, D_MODEL), F32),
        compiler_params=pltpu.CompilerParams(
            dimension_semantics=("arbitrary",), vmem_limit_bytes=VMEM_LIMIT),
        name="final",
    )(key, row_off, h2, ys8, g_final)


def _prep_attention_weights(w_in, w_uq, w_ukv):
    d = w_in.shape[0]
    c0 = MLA_Q_RANK + MLA_KV_RANK
    kpe = w_in[:, c0:c0 + MLA_QK_ROPE]
    kpe_blk = jnp.concatenate(
        [jnp.zeros((d, MLA_QK_NOPE), F32), kpe, jnp.zeros((d, LANES - MLA_QK_NOPE - MLA_QK_ROPE), F32)], axis=1)
    w_in_r = jnp.concatenate([w_in[:, :c0], kpe_blk, w_in[:, c0 + MLA_QK_ROPE:]], axis=1).astype(BF16)

    dqk = MLA_QK_NOPE + MLA_QK_ROPE
    uq = w_uq.reshape(MLA_Q_RANK, MLA_HEADS, dqk)
    uq = jnp.pad(uq, ((0, 0), (0, 0), (0, LANES - dqk))).reshape(MLA_Q_RANK, MLA_HEADS * LANES).astype(BF16)

    ukv = w_ukv.reshape(MLA_KV_RANK, MLA_HEADS, MLA_QK_NOPE + MLA_V_DIM)
    uk = jnp.pad(ukv[:, :, :MLA_QK_NOPE], ((0, 0), (0, 0), (0, LANES - MLA_QK_NOPE)))
    uk = uk.reshape(MLA_KV_RANK, MLA_HEADS * LANES)
    uv = ukv[:, :, MLA_QK_NOPE:].reshape(MLA_KV_RANK, MLA_HEADS * MLA_V_DIM)
    w_ukv_r = jnp.concatenate([uk, uv], axis=1).astype(BF16)
    return w_in_r, uq, w_ukv_r


def _rope_freq_lanes():
    fb = ROPE_THETA ** (-jnp.arange(DIFF_HALF // 2, dtype=F32) / (DIFF_HALF // 2))
    fa = ROPE_THETA ** (-jnp.arange(MLA_QK_ROPE // 2, dtype=F32) / (MLA_QK_ROPE // 2))
    return jnp.concatenate([fb, fb, fa, fa, jnp.zeros((LANES - 96,), F32)])[None, :]


def kernel(x, positions, attn_norm_g, w_in, q_norm_g, w_uq, kv_norm_g, w_ukv, lambda_q1, lambda_k1, lambda_q2, lambda_k2, subln_g, w_o, ffn_norm_g, w_router_group, b_router_group, w_router_expert, b_router_expert, w_gate, w_up, w_down, final_norm_g):
    B, S, D = x.shape
    T = B * S
    l = 0

    w_in_r, w_uq_r, w_ukv_r = _prep_attention_weights(w_in[l], w_uq[l], w_ukv[l])
    pos_f = positions.astype(F32)[:, :, None]
    qa, ka, va = _run_proj(x, pos_f, _rope_freq_lanes(), attn_norm_g[l][None, :], w_in_r,
                           q_norm_g[l][None, :], w_uq_r, kv_norm_g[l][None, :], w_ukv_r)

    lam_vecs = jnp.stack([lambda_q1[l], lambda_k1[l], lambda_q2[l], lambda_k2[l]]).astype(F32)
    o = _run_attn(qa, ka, va, lam_vecs, subln_g[l][None, :])

    w_r = jnp.concatenate(
        [w_router_group[l], w_router_expert[l].reshape(D, N_EXPERTS),
         jnp.zeros((D, LANES - N_GROUPS - N_EXPERTS), F32)], axis=1).astype(BF16)
    b_r = jnp.concatenate(
        [b_router_group[l], b_router_expert[l].reshape(N_EXPERTS),
         jnp.zeros((LANES - N_GROUPS - N_EXPERTS,), F32)])[None, :].astype(F32)
    ii = jnp.arange(TM_POST)
    tri = (ii[None, :] < ii[:, None]).astype(BF16)
    x2 = x.reshape(T, D)
    h2, tp8, route, counts = _run_post(o.reshape(T, D), x2, w_o[l].astype(BF16), ffn_norm_g[l][None, :],
                                       w_r, b_r, tri)

    pad_tiles = N_GROUPS * (TILES_PER_STEP - 1)
    n_tiles_max = T // TM_MOE + N_CLASSES - 1 + pad_tiles + GATHER_AHEAD
    n_tiles_max += -n_tiles_max % TILES_PER_STEP
    key = route[:, 0].astype(jnp.int32) * T + route[:, 1].astype(jnp.int32)
    cnt = counts[0, :N_CLASSES].astype(jnp.int32)
    tiles_gc = ((cnt + TM_MOE - 1) // TM_MOE).reshape(N_GROUPS, PAIRS_PER_GROUP)
    group_pad = -jnp.sum(tiles_gc, axis=1) % TILES_PER_STEP
    tiles_c = tiles_gc.at[:, -1].add(group_pad).reshape(N_CLASSES)
    tile_end = jnp.cumsum(tiles_c)
    n_tiles = tile_end[-1]
    row_off = jnp.concatenate([(tile_end - tiles_c) * TM_MOE, (n_tiles * TM_MOE)[None],
                               jnp.zeros((LANES - N_CLASSES - 1,), jnp.int32)])
    tile_ids = jnp.arange(n_tiles_max, dtype=jnp.int32)
    tile_cls = jnp.sum(jnp.minimum(tile_ids, n_tiles - 1)[:, None] >= tile_end[None, :], axis=1, dtype=jnp.int32)
    pair_a, pair_b = [], []
    for g in range(N_GROUPS):
        for a in range(EXPERTS_PER_GROUP):
            for b in range(a + 1, EXPERTS_PER_GROUP):
                pair_a.append(g * EXPERTS_PER_GROUP + a)
                pair_b.append(g * EXPERTS_PER_GROUP + b)
    cls_onehot = (tile_cls[:, None] == jnp.arange(N_CLASSES, dtype=jnp.int32)[None, :]).astype(jnp.int32)
    tile_a = cls_onehot @ jnp.array(pair_a, jnp.int32)
    tile_b = cls_onehot @ jnp.array(pair_b, jnp.int32)

    ys8 = _run_moe(key, row_off, tile_a, tile_b, tp8, w_r, b_r,
                   w_gate[l].astype(BF16), w_up[l].astype(BF16), w_down[l].astype(BF16))
    out = _run_final(key, row_off, h2, ys8, final_norm_g[None, :])
    return out.reshape(B, S, D)
```

```python
import functools
import math

import jax
import jax.numpy as jnp
from jax import lax
from jax.experimental import pallas as pl
from jax.experimental.pallas import tpu as pltpu

F32 = jnp.float32
BF16 = jnp.bfloat16

D_MODEL = 1024
MLA_HEADS = 8
MLA_QK_NOPE = 64
MLA_QK_ROPE = 32
MLA_V_DIM = 64
MLA_Q_RANK = 384
MLA_KV_RANK = 256
DIFF_HEADS = 4
DIFF_HALF = 64
DIFF_V_DIM = 2 * DIFF_HALF
N_GROUPS = 4
EXPERTS_PER_GROUP = 8
N_EXPERTS = N_GROUPS * EXPERTS_PER_GROUP
D_FF_EXPERT = 256
ROPE_THETA = 10000.0
EPS = 1e-6
LAM_INIT = 0.8 - 0.6 * math.exp(-0.3 * 0)

LANES = 128
N_QK_BLOCKS = MLA_HEADS + DIFF_HEADS
N_V_BLOCKS = MLA_HEADS // 2 + DIFF_HEADS
N_UNITS = MLA_HEADS // 2 + DIFF_HEADS
PROJ_COLS = MLA_Q_RANK + MLA_KV_RANK + LANES + 3 * DIFF_HEADS * DIFF_V_DIM
LOG2E = 1.4426950408889634
NEG_BIG = -1e30

TM_PROJ = 1024
PROJ_SPLIT = 4
TQ = 256
TK = 256
SCORE_LEAD = 6
KV_UNROLL = 4
TM_POST = 1024
POST_SPLIT = 4
TM_MOE = 128
TM_FIN = 256
SUBLANES = 8
PAIRS_PER_GROUP = EXPERTS_PER_GROUP * (EXPERTS_PER_GROUP - 1) // 2
N_CLASSES = N_GROUPS * PAIRS_PER_GROUP
VMEM_LIMIT = 48 * 1024 * 1024


def _rms(x, g):
    return x * lax.rsqrt(jnp.mean(x * x, axis=-1, keepdims=True) + EPS) * g


def _proj_kernel(x_ref, pos_ref, freq_ref, g_attn_ref, w_in_ref, gq_ref, w_uq_ref, gkv_ref, w_ukv_ref,
                 qa_ref, ka_ref, va_ref):
    tm = x_ref.shape[1] // PROJ_SPLIT
    parts = [slice(part * tm, (part + 1) * tm) for part in range(PROJ_SPLIT)]
    projs = [jnp.dot(_rms(x_ref[0, rows], g_attn_ref[...]).astype(BF16), w_in_ref[...],
                     preferred_element_type=F32) for rows in parts]
    ups = []
    for proj in projs:
        c_q = proj[:, :MLA_Q_RANK]
        c_kv = proj[:, MLA_Q_RANK:MLA_Q_RANK + MLA_KV_RANK]
        ups.append((jnp.dot(_rms(c_q, gq_ref[...]).astype(BF16), w_uq_ref[...], preferred_element_type=F32),
                    jnp.dot(_rms(c_kv, gkv_ref[...]).astype(BF16), w_ukv_ref[...], preferred_element_type=F32)))
    for rows, proj, (q, kv) in zip(parts, projs, ups):
        _rope_and_store(rows, proj, q, kv, pos_ref, freq_ref, qa_ref, ka_ref, va_ref)


def _rope_and_store(rows, proj, q, kv, pos_ref, freq_ref, qa_ref, ka_ref, va_ref):
    ang = pos_ref[0, rows] * freq_ref[...]
    cos_t, sin_t = jnp.cos(ang), jnp.sin(ang)
    lane = lax.broadcasted_iota(jnp.int32, ang.shape, 1)
    in_rope = (lane >= 64) & (lane < 96)
    cos_a = jnp.where(in_rope, cos_t, 1.0)
    sin_a_hi = jnp.where((lane >= 80) & (lane < 96), sin_t, 0.0)
    sin_a_lo = jnp.where((lane >= 64) & (lane < 80), -sin_t, 0.0)
    cos_b = jnp.where(lane < 64, cos_t, pltpu.roll(cos_t, 64, 1))
    sin_b = jnp.where(lane < 64, sin_t, pltpu.roll(sin_t, 64, 1))
    sub = lane % 64
    sin_b_hi = jnp.where(sub >= 32, sin_b, 0.0)
    sin_b_lo = jnp.where(sub < 32, -sin_b, 0.0)

    def rope_a(xb, scale):
        return (xb * (cos_a * scale) + pltpu.roll(xb, 16, 1) * (sin_a_hi * scale)
                + pltpu.roll(xb, 112, 1) * (sin_a_lo * scale))

    def rope_b(xb, scale):
        return (xb * (cos_b * scale) + pltpu.roll(xb, 32, 1) * (sin_b_hi * scale)
                + pltpu.roll(xb, 96, 1) * (sin_b_lo * scale))

    base = MLA_Q_RANK + MLA_KV_RANK
    kpe = rope_a(proj[:, base:base + LANES], 1.0)
    base += LANES

    scale_mla = (MLA_QK_NOPE + MLA_QK_ROPE) ** -0.5 * LOG2E
    scale_diff = DIFF_HALF ** -0.5 * LOG2E
    for h in range(MLA_HEADS):
        sl = slice(h * LANES, (h + 1) * LANES)
        qa_ref[0, h, rows] = rope_a(q[:, sl], scale_mla).astype(BF16)
        ka_ref[0, h, rows] = (kv[:, sl] + kpe).astype(BF16)
    vbase = MLA_HEADS * LANES
    for hp in range(MLA_HEADS // 2):
        va_ref[0, hp, :, rows] = kv[:, vbase + hp * LANES: vbase + (hp + 1) * LANES].T.astype(BF16)
    width = DIFF_HEADS * DIFF_V_DIM
    for h in range(DIFF_HEADS):
        sl = slice(base + h * LANES, base + (h + 1) * LANES)
        qa_ref[0, MLA_HEADS + h, rows] = rope_b(proj[:, sl], scale_diff).astype(BF16)
        sl = slice(base + width + h * LANES, base + width + (h + 1) * LANES)
        ka_ref[0, MLA_HEADS + h, rows] = rope_b(proj[:, sl], 1.0).astype(BF16)
        sl = slice(base + 2 * width + h * LANES, base + 2 * width + (h + 1) * LANES)
        va_ref[0, MLA_HEADS // 2 + h, :, rows] = proj[:, sl].T.astype(BF16)


def _run_proj(x, pos_f, freq, g_attn, w_in_r, gq, w_uq_r, gkv, w_ukv_r):
    B, S, _ = x.shape
    tm = TM_PROJ
    full = lambda shape: pl.BlockSpec(shape, lambda b, i: (0,) * len(shape))
    head_out = lambda n: pl.BlockSpec((1, n, tm, LANES), lambda b, i: (b, 0, i, 0))
    return pl.pallas_call(
        _proj_kernel,
        grid=(B, S // tm),
        in_specs=[
            pl.BlockSpec((1, tm, D_MODEL), lambda b, i: (b, i, 0)),
            pl.BlockSpec((1, tm, 1), lambda b, i: (b, i, 0)),
            full((1, LANES)),
            full((1, D_MODEL)),
            full((D_MODEL, PROJ_COLS)),
            full((1, MLA_Q_RANK)),
            full((MLA_Q_RANK, MLA_HEADS * LANES)),
            full((1, MLA_KV_RANK)),
            full((MLA_KV_RANK, MLA_HEADS * LANES + MLA_HEADS * MLA_V_DIM)),
        ],
        out_specs=[head_out(N_QK_BLOCKS), head_out(N_QK_BLOCKS),
                   pl.BlockSpec((1, N_V_BLOCKS, LANES, tm), lambda b, i: (b, 0, 0, i))],
        out_shape=[
            jax.ShapeDtypeStruct((B, N_QK_BLOCKS, S, LANES), BF16),
            jax.ShapeDtypeStruct((B, N_QK_BLOCKS, S, LANES), BF16),
            jax.ShapeDtypeStruct((B, N_V_BLOCKS, LANES, S), BF16),
        ],
        compiler_params=pltpu.CompilerParams(
            dimension_semantics=("arbitrary", "arbitrary"), vmem_limit_bytes=VMEM_LIMIT),
        name="proj",
    )(x, pos_f, freq, g_attn, w_in_r, gq, w_uq_r, gkv, w_ukv_r)


def _attn_kernel(qa_ref, ka_ref, vt_ref, lam_ref, subln_ref, o_ref, m_ref, l_ref, acc_ref):
    qi = pl.program_id(1)
    m_ref[...] = jnp.full(m_ref.shape, NEG_BIG, F32)
    l_ref[...] = jnp.zeros(l_ref.shape, F32)
    acc_ref[...] = jnp.zeros(acc_ref.shape, F32)

    low_half = lax.broadcasted_iota(jnp.int32, (TQ, LANES), 1) < 64
    is_mla = lambda u: u < MLA_HEADS // 2
    v_rows = lambda u, s: slice(s * MLA_V_DIM, (s + 1) * MLA_V_DIM) if is_mla(u) else slice(0, DIFF_V_DIM)

    def unit_operands(u):
        if is_mla(u):
            return (qa_ref[0, 2 * u], 2 * u), (qa_ref[0, 2 * u + 1], 2 * u + 1), u
        h = u - MLA_HEADS // 2
        q = qa_ref[0, MLA_HEADS + h]
        zero = jnp.zeros_like(q)
        return ((jnp.where(low_half, q, zero), MLA_HEADS + h),
                (jnp.where(low_half, zero, q), MLA_HEADS + h), MLA_HEADS // 2 + h)

    def scores(q, k, mask):
        st = lax.dot_general(k, q, (((1,), (1,)), ((), ())), preferred_element_type=F32)
        return st if mask is None else jnp.where(mask, st, NEG_BIG)

    def softmax_update(slot, rows, st, vt):
        m_prev = m_ref[slot]
        m_new = jnp.maximum(m_prev, jnp.max(st, axis=0, keepdims=True))
        alpha = jnp.exp2(m_prev - m_new)
        pt = jnp.exp2(st - m_new)
        l_ref[slot] = alpha * l_ref[slot] + jnp.sum(pt, axis=0, keepdims=True)
        acc_ref[slot, rows] = alpha * acc_ref[slot, rows] + jnp.dot(vt[rows], pt.astype(BF16),
                                                                    preferred_element_type=F32)
        m_ref[slot] = m_new

    streams = []
    for u in range(N_UNITS):
        (q0, k0), (q1, k1), vi = unit_operands(u)
        streams += [(2 * u, v_rows(u, 0), q0, k0, vi), (2 * u + 1, v_rows(u, 1), q1, k1, vi)]

    def kv_tiles(tiles):
        work = [(j, mask, st) for j, mask in tiles for st in streams]
        pending = {}
        for n in range(len(work) + SCORE_LEAD):
            if n < len(work):
                j, mask, (_, _, q, kidx, _) = work[n]
                keys = pl.ds(pl.multiple_of(j * TK, TK), TK)
                pending[n] = scores(q, ka_ref[0, kidx, keys, :], mask)
            if n >= SCORE_LEAD:
                j, _, (slot, rows, _, _, vi) = work[n - SCORE_LEAD]
                keys = pl.ds(pl.multiple_of(j * TK, TK), TK)
                softmax_update(slot, rows, pending.pop(n - SCORE_LEAD), vt_ref[0, vi, :, keys])

    def body(jj, carry):
        kv_tiles([(KV_UNROLL * jj + d, None) for d in range(KV_UNROLL)])
        return carry

    lax.fori_loop(0, qi // KV_UNROLL, body, 0)
    key_pos = lax.broadcasted_iota(jnp.int32, (TK, TQ), 0)
    query_pos = lax.broadcasted_iota(jnp.int32, (TK, TQ), 1)
    causal = key_pos <= query_pos
    for rest in range(KV_UNROLL):
        @pl.when(qi % KV_UNROLL == rest)
        def _():
            kv_tiles([(qi - rest + d, None) for d in range(rest)] + [(qi, causal)])

    lam_v = lam_ref[...]
    lam = (jnp.exp(jnp.sum(lam_v[0:1] * lam_v[1:2], axis=1, keepdims=True))
           - jnp.exp(jnp.sum(lam_v[2:3] * lam_v[3:4], axis=1, keepdims=True)) + LAM_INIT)
    for u in range(N_UNITS):
        r0, r1 = v_rows(u, 0), v_rows(u, 1)
        o0 = acc_ref[2 * u, r0] / l_ref[2 * u]
        o1 = acc_ref[2 * u + 1, r1] / l_ref[2 * u + 1]
        if is_mla(u):
            out = jnp.concatenate([o0, o1], axis=0).T
            col0 = u * LANES
        else:
            d = o0 - lam * o1
            d = d * lax.rsqrt(jnp.mean(d * d, axis=0, keepdims=True) + EPS)
            out = d.T * subln_ref[...] * (1.0 - LAM_INIT)
            col0 = MLA_HEADS * MLA_V_DIM + (u - MLA_HEADS // 2) * LANES
        o_ref[0, :, col0:col0 + LANES] = out.astype(BF16)


def _run_attn(qa, ka, vt, lam_vecs, subln):
    B, _, S, _ = qa.shape
    n_slots = 2 * N_UNITS
    return pl.pallas_call(
        _attn_kernel,
        grid=(B, S // TQ),
        in_specs=[
            pl.BlockSpec((1, N_QK_BLOCKS, TQ, LANES), lambda b, i: (b, 0, i, 0)),
            pl.BlockSpec((1, N_QK_BLOCKS, S, LANES), lambda b, i: (b, 0, 0, 0)),
            pl.BlockSpec((1, N_V_BLOCKS, LANES, S), lambda b, i: (b, 0, 0, 0)),
            pl.BlockSpec((4, DIFF_HALF), lambda b, i: (0, 0)),
            pl.BlockSpec((1, DIFF_V_DIM), lambda b, i: (0, 0)),
        ],
        out_specs=pl.BlockSpec((1, TQ, D_MODEL), lambda b, i: (b, i, 0)),
        out_shape=jax.ShapeDtypeStruct((B, S, D_MODEL), BF16),
        scratch_shapes=[
            pltpu.VMEM((n_slots, 1, TQ), F32),
            pltpu.VMEM((n_slots, 1, TQ), F32),
            pltpu.VMEM((n_slots, LANES, TQ), F32),
        ],
        compiler_params=pltpu.CompilerParams(
            dimension_semantics=("arbitrary", "arbitrary"), vmem_limit_bytes=VMEM_LIMIT),
        name="attn",
    )(qa, ka, vt, lam_vecs, subln)


def _post_kernel(o_ref, x_ref, w_o_ref, g_ffn_ref, w_r_ref, b_r_ref, tri_ref,
                 h_ref, tp_ref, route_ref, cnt_ref):
    tm = x_ref.shape[0]

    @pl.when(pl.program_id(0) == 0)
    def _():
        cnt_ref[...] = jnp.zeros(cnt_ref.shape, F32)

    tp_rows = tm // POST_SPLIT
    parts = [slice(p * tp_rows, (p + 1) * tp_rows) for p in range(POST_SPLIT)]
    ts = []
    for rows in parts:
        h = x_ref[rows] + jnp.dot(o_ref[rows], w_o_ref[...], preferred_element_type=F32)
        h_ref[rows] = h
        ts.append(_rms(h, g_ffn_ref[...]))
    logits = [jnp.dot(t.astype(BF16), w_r_ref[...], preferred_element_type=F32) + b_r_ref[...] for t in ts]
    routed = [_route_class(lg) for lg in logits]

    seen = cnt_ref[...]
    for rows, (cls, onehot) in zip(parts, routed):
        before = jnp.dot(tri_ref[...], onehot.astype(BF16), preferred_element_type=F32) + seen
        rank = jnp.sum(before * onehot, axis=1, keepdims=True)
        seen = seen + jnp.sum(onehot, axis=0, keepdims=True)
        lane = lax.broadcasted_iota(jnp.int32, onehot.shape, 1)
        route_ref[rows] = jnp.where(lane == 0, cls, jnp.where(lane == 1, rank, 0.0))
    cnt_ref[...] = seen

    for rows, t in zip(parts, ts):
        for c in range(SUBLANES):
            tp_ref[pl.ds(rows.start * SUBLANES + c, tp_rows, stride=SUBLANES), :] = t[:, c * LANES:(c + 1) * LANES]


def _route_class(lg):
    lane = lax.broadcasted_iota(jnp.int32, lg.shape, 1)
    lane_f = lane.astype(F32)
    first_lane = lambda hit: jnp.min(jnp.where(hit, lane_f, float(LANES)), axis=1, keepdims=True)

    is_group = lane < N_GROUPS
    gl = jnp.where(is_group, lg, NEG_BIG)
    gmax = jnp.max(gl, axis=1, keepdims=True)
    gidx = first_lane(gl == gmax)

    lo = N_GROUPS + EXPERTS_PER_GROUP * gidx
    el = jnp.where((lane_f >= lo) & (lane_f < lo + EXPERTS_PER_GROUP), lg, NEG_BIG)
    m1 = jnp.max(el, axis=1, keepdims=True)
    i1 = first_lane(el == m1)
    el2 = jnp.where(lane_f == i1, NEG_BIG, el)
    m2 = jnp.max(el2, axis=1, keepdims=True)
    i2 = first_lane(el2 == m2)

    a = jnp.minimum(i1, i2) - lo
    b = jnp.maximum(i1, i2) - lo
    pair = a * (2 * EXPERTS_PER_GROUP - 1 - a) * 0.5 + (b - a - 1.0)
    cls = gidx * PAIRS_PER_GROUP + pair
    return cls, jnp.where(lane_f == cls, 1.0, 0.0)


def _run_post(o2, x2, w_o, g_ffn, w_r, b_r, tri):
    T = x2.shape[0]
    tm = TM_POST
    full = lambda shape: pl.BlockSpec(shape, lambda i: (0,) * len(shape))
    rows = lambda w: pl.BlockSpec((tm, w), lambda i: (i, 0))
    return pl.pallas_call(
        _post_kernel,
        grid=(T // tm,),
        in_specs=[rows(D_MODEL), rows(D_MODEL), full((D_MODEL, D_MODEL)), full((1, D_MODEL)),
                  full((D_MODEL, LANES)), full((1, LANES)), full((tm // POST_SPLIT, tm // POST_SPLIT))],
        out_specs=[rows(D_MODEL), pl.BlockSpec((tm * SUBLANES, LANES), lambda i: (i, 0)), rows(LANES),
                   full((1, LANES))],
        out_shape=[
            jax.ShapeDtypeStruct((T, D_MODEL), F32),
            jax.ShapeDtypeStruct((T * SUBLANES, LANES), F32),
            jax.ShapeDtypeStruct((T, LANES), F32),
            jax.ShapeDtypeStruct((1, LANES), F32),
        ],
        compiler_params=pltpu.CompilerParams(
            dimension_semantics=("arbitrary",), vmem_limit_bytes=VMEM_LIMIT),
        name="post",
    )(o2, x2, w_o, g_ffn, w_r, b_r, tri)


TILE_ROWS = TM_MOE * SUBLANES


TILES_PER_STEP = 4
GATHER_AHEAD = 2
GATHER_SLOTS = GATHER_AHEAD + 1


def _sorted_row(key_ref, off_ref, t):
    n_tok = key_ref.shape[0]
    key = key_ref[t]
    return off_ref[key >> (n_tok.bit_length() - 1)] + (key & (n_tok - 1))


def _moe_kernel(key_ref, off_ref, ta_ref, tb_ref, tp_hbm, w_r_ref, b_r_ref, wg_ref, wu_ref, wd_ref,
                y_ref, src_ref, xbuf, sem):
    step = pl.program_id(0)
    n_tok = key_ref.shape[0]
    n_used = off_ref[N_CLASSES] // TM_MOE

    def start_gather(tile):
        slot = tile % GATHER_SLOTS
        for r in range(TM_MOE):
            tok = src_ref[tile * TM_MOE + r]
            pltpu.make_async_copy(
                tp_hbm.at[pl.ds(pl.multiple_of(tok * SUBLANES, SUBLANES), SUBLANES), :],
                xbuf.at[pl.ds(pl.multiple_of(slot * TILE_ROWS + r * SUBLANES, SUBLANES), SUBLANES), :],
                sem.at[slot]).start(priority=r % 2)

    def wait_gather(tile):
        slot = tile % GATHER_SLOTS
        pltpu.make_async_copy(
            tp_hbm.at[pl.ds(0, TILE_ROWS), :],
            xbuf.at[pl.ds(pl.multiple_of(slot * TILE_ROWS, SUBLANES), TILE_ROWS), :],
            sem.at[slot]).wait()

    @pl.when(step == 0)
    def _():
        def fill(tile, c):
            for r in range(TM_MOE):
                src_ref[tile * TM_MOE + r] = (tile * TM_MOE + r) & (n_tok - 1)
            return c

        def invert(t, c):
            src_ref[_sorted_row(key_ref, off_ref, t)] = t
            return c

        lax.fori_loop(0, n_used + GATHER_AHEAD, fill, 0)
        lax.fori_loop(0, n_tok, invert, 0, unroll=16)
        for tile in range(GATHER_AHEAD):
            start_gather(tile)

    for sub in range(TILES_PER_STEP):
        _moe_tile(step * TILES_PER_STEP + sub, sub, n_used, ta_ref, tb_ref, w_r_ref, b_r_ref,
                  wg_ref, wu_ref, wd_ref, y_ref, xbuf, start_gather, wait_gather)


def _moe_tile(i, sub, n_used, ta_ref, tb_ref, w_r_ref, b_r_ref, wg_ref, wu_ref, wd_ref, y_ref, xbuf,
              start_gather, wait_gather):
    out0 = sub * TILE_ROWS

    @pl.when(i < n_used)
    def _():
        wait_gather(i)
        base = (i % GATHER_SLOTS) * TILE_ROWS
        x = jnp.concatenate([xbuf[pl.ds(base + c, TM_MOE, stride=SUBLANES), :] for c in range(SUBLANES)],
                            axis=1).astype(BF16)

        lg = jnp.dot(x, w_r_ref[...], preferred_element_type=F32) + b_r_ref[...]
        lane = lax.broadcasted_iota(jnp.int32, lg.shape, 1)
        pick = lambda idx: jnp.sum(jnp.where(lane == idx, lg, 0.0), axis=1, keepdims=True)
        ea, eb = ta_ref[i], tb_ref[i]
        g_logit = pick(ea // EXPERTS_PER_GROUP)
        p_group = 1.0 / jnp.sum(jnp.where(lane < N_GROUPS, jnp.exp(lg - g_logit), 0.0), axis=1, keepdims=True)
        l_a, l_b = pick(N_GROUPS + ea), pick(N_GROUPS + eb)
        gate_a = p_group / (1.0 + jnp.exp(l_b - l_a))
        gate_b = p_group / (1.0 + jnp.exp(l_a - l_b))

        def hidden(e_loc, gate):
            a = jnp.dot(x, wg_ref[e_loc], preferred_element_type=F32)
            u = jnp.dot(x, wu_ref[e_loc], preferred_element_type=F32)
            return ((a * jax.nn.sigmoid(a)) * u * gate).astype(BF16)

        a_loc, b_loc = ea % EXPERTS_PER_GROUP, eb % EXPERTS_PER_GROUP
        y = (jnp.dot(hidden(a_loc, gate_a), wd_ref[a_loc], preferred_element_type=F32)
             + jnp.dot(hidden(b_loc, gate_b), wd_ref[b_loc], preferred_element_type=F32))
        for c in range(SUBLANES):
            y_ref[pl.ds(out0 + c, TM_MOE, stride=SUBLANES), :] = y[:, c * LANES:(c + 1) * LANES]
        start_gather(i + GATHER_AHEAD)

    @pl.when(i >= n_used)
    def _():
        y_ref[pl.ds(out0, TILE_ROWS), :] = jnp.zeros((TILE_ROWS, LANES), F32)

        @pl.when(i < n_used + GATHER_AHEAD)
        def _():
            wait_gather(i)


def _run_moe(key, row_off, tile_a, tile_b, tp8, w_r, b_r, wg, wu, wd):
    n_tiles = tile_a.shape[0]
    n_tok = key.shape[0]
    assert n_tok & (n_tok - 1) == 0, "token count must be a power of two (class/rank key packing)"
    assert n_tiles % TILES_PER_STEP == 0
    group = lambda i, k, o, ta, tb: (ta[i * TILES_PER_STEP] // EXPERTS_PER_GROUP, 0, 0)
    grid_spec = pltpu.PrefetchScalarGridSpec(
        num_scalar_prefetch=4,
        grid=(n_tiles // TILES_PER_STEP,),
        in_specs=[
            pl.BlockSpec(memory_space=pl.ANY),
            pl.BlockSpec((D_MODEL, LANES), lambda i, k, o, ta, tb: (0, 0)),
            pl.BlockSpec((1, LANES), lambda i, k, o, ta, tb: (0, 0)),
            pl.BlockSpec((EXPERTS_PER_GROUP, D_MODEL, D_FF_EXPERT), group),
            pl.BlockSpec((EXPERTS_PER_GROUP, D_MODEL, D_FF_EXPERT), group),
            pl.BlockSpec((EXPERTS_PER_GROUP, D_FF_EXPERT, D_MODEL), group),
        ],
        out_specs=pl.BlockSpec((TILES_PER_STEP * TILE_ROWS, LANES), lambda i, k, o, ta, tb: (i, 0)),
        scratch_shapes=[
            pltpu.SMEM((n_tiles * TM_MOE,), jnp.int32),
            pltpu.VMEM((GATHER_SLOTS * TILE_ROWS, LANES), F32),
            pltpu.SemaphoreType.DMA((GATHER_SLOTS,)),
        ],
    )
    return pl.pallas_call(
        _moe_kernel,
        grid_spec=grid_spec,
        out_shape=jax.ShapeDtypeStruct((n_tiles * TILE_ROWS, LANES), F32),
        compiler_params=pltpu.CompilerParams(
            dimension_semantics=("arbitrary",), vmem_limit_bytes=VMEM_LIMIT),
        name="moe",
    )(key, row_off, tile_a, tile_b, tp8, w_r, b_r, wg, wu, wd)


FIN_ROWS = TM_FIN * SUBLANES


def _final_kernel(key_ref, off_ref, h_ref, y_hbm, g_ref, out_ref, ybuf, sem):
    i = pl.program_id(0)
    slot = i % 2

    def start_gather(tile, slot):
        def body(r2, c):
            for q in range(2):
                r = 2 * r2 + q
                p = _sorted_row(key_ref, off_ref, tile * TM_FIN + r)
                pltpu.make_async_copy(
                    y_hbm.at[pl.ds(pl.multiple_of(p * SUBLANES, SUBLANES), SUBLANES), :],
                    ybuf.at[pl.ds(pl.multiple_of(slot * FIN_ROWS + r * SUBLANES, SUBLANES), SUBLANES), :],
                    sem.at[slot]).start(priority=q)
            return c
        lax.fori_loop(0, TM_FIN // 2, body, 0, unroll=4)

    @pl.when(i == 0)
    def _():
        start_gather(0, 0)

    @pl.when(i + 1 < pl.num_programs(0))
    def _():
        start_gather(i + 1, 1 - slot)

    pltpu.make_async_copy(
        y_hbm.at[pl.ds(0, FIN_ROWS), :],
        ybuf.at[pl.ds(pl.multiple_of(slot * FIN_ROWS, SUBLANES), FIN_ROWS), :],
        sem.at[slot]).wait()
    base = slot * FIN_ROWS
    y = jnp.concatenate([ybuf[pl.ds(base + c, TM_FIN, stride=SUBLANES), :] for c in range(SUBLANES)], axis=1)
    out_ref[...] = _rms(h_ref[...] + y, g_ref[...])


def _run_final(key, row_off, h2, ys8, g_final):
    T = h2.shape[0]
    grid_spec = pltpu.PrefetchScalarGridSpec(
        num_scalar_prefetch=2,
        grid=(T // TM_FIN,),
        in_specs=[
            pl.BlockSpec((TM_FIN, D_MODEL), lambda i, k, o: (i, 0)),
            pl.BlockSpec(memory_space=pl.ANY),
            pl.BlockSpec((1, D_MODEL), lambda i, k, o: (0, 0)),
        ],
        out_specs=pl.BlockSpec((TM_FIN, D_MODEL), lambda i, k, o: (i, 0)),
        scratch_shapes=[pltpu.VMEM((2 * FIN_ROWS, LANES), F32), pltpu.SemaphoreType.DMA((2,))],
    )
    return pl.pallas_call(
        _final_kernel,
        grid_spec=grid_spec,
        out_shape=jax.ShapeDtypeStruct((T, D_MODEL), F32),
        compiler_params=pltpu.CompilerParams(
            dimension_semantics=("arbitrary",), vmem_limit_bytes=VMEM_LIMIT),
        name="final",
    )(key, row_off, h2, ys8, g_final)


def _prep_attention_weights(w_in, w_uq, w_ukv):
    d = w_in.shape[0]
    c0 = MLA_Q_RANK + MLA_KV_RANK
    kpe = w_in[:, c0:c0 + MLA_QK_ROPE]
    kpe_blk = jnp.concatenate(
        [jnp.zeros((d, MLA_QK_NOPE), F32), kpe, jnp.zeros((d, LANES - MLA_QK_NOPE - MLA_QK_ROPE), F32)], axis=1)
    w_in_r = jnp.concatenate([w_in[:, :c0], kpe_blk, w_in[:, c0 + MLA_QK_ROPE:]], axis=1).astype(BF16)

    dqk = MLA_QK_NOPE + MLA_QK_ROPE
    uq = w_uq.reshape(MLA_Q_RANK, MLA_HEADS, dqk)
    uq = jnp.pad(uq, ((0, 0), (0, 0), (0, LANES - dqk))).reshape(MLA_Q_RANK, MLA_HEADS * LANES).astype(BF16)

    ukv = w_ukv.reshape(MLA_KV_RANK, MLA_HEADS, MLA_QK_NOPE + MLA_V_DIM)
    uk = jnp.pad(ukv[:, :, :MLA_QK_NOPE], ((0, 0), (0, 0), (0, LANES - MLA_QK_NOPE)))
    uk = uk.reshape(MLA_KV_RANK, MLA_HEADS * LANES)
    uv = ukv[:, :, MLA_QK_NOPE:].reshape(MLA_KV_RANK, MLA_HEADS * MLA_V_DIM)
    w_ukv_r = jnp.concatenate([uk, uv], axis=1).astype(BF16)
    return w_in_r, uq, w_ukv_r


def _rope_freq_lanes():
    fb = ROPE_THETA ** (-jnp.arange(DIFF_HALF // 2, dtype=F32) / (DIFF_HALF // 2))
    fa = ROPE_THETA ** (-jnp.arange(MLA_QK_ROPE // 2, dtype=F32) / (MLA_QK_ROPE // 2))
    return jnp.concatenate([fb, fb, fa, fa, jnp.zeros((LANES - 96,), F32)])[None, :]


def kernel(x, positions, attn_norm_g, w_in, q_norm_g, w_uq, kv_norm_g, w_ukv, lambda_q1, lambda_k1, lambda_q2, lambda_k2, subln_g, w_o, ffn_norm_g, w_router_group, b_router_group, w_router_expert, b_router_expert, w_gate, w_up, w_down, final_norm_g):
    B, S, D = x.shape
    T = B * S
    l = 0

    w_in_r, w_uq_r, w_ukv_r = _prep_attention_weights(w_in[l], w_uq[l], w_ukv[l])
    pos_f = positions.astype(F32)[:, :, None]
    qa, ka, va = _run_proj(x, pos_f, _rope_freq_lanes(), attn_norm_g[l][None, :], w_in_r,
                           q_norm_g[l][None, :], w_uq_r, kv_norm_g[l][None, :], w_ukv_r)

    lam_vecs = jnp.stack([lambda_q1[l], lambda_k1[l], lambda_q2[l], lambda_k2[l]]).astype(F32)
    o = _run_attn(qa, ka, va, lam_vecs, subln_g[l][None, :])

    w_r = jnp.concatenate(
        [w_router_group[l], w_router_expert[l].reshape(D, N_EXPERTS),
         jnp.zeros((D, LANES - N_GROUPS - N_EXPERTS), F32)], axis=1).astype(BF16)
    b_r = jnp.concatenate(
        [b_router_group[l], b_router_expert[l].reshape(N_EXPERTS),
         jnp.zeros((LANES - N_GROUPS - N_EXPERTS,), F32)])[None, :].astype(F32)
    ii = jnp.arange(TM_POST // POST_SPLIT)
    tri = (ii[None, :] < ii[:, None]).astype(BF16)
    x2 = x.reshape(T, D)
    h2, tp8, route, counts = _run_post(o.reshape(T, D), x2, w_o[l].astype(BF16), ffn_norm_g[l][None, :],
                                       w_r, b_r, tri)

    pad_tiles = N_GROUPS * (TILES_PER_STEP - 1)
    n_tiles_max = T // TM_MOE + N_CLASSES - 1 + pad_tiles + GATHER_AHEAD
    n_tiles_max += -n_tiles_max % TILES_PER_STEP
    key = route[:, 0].astype(jnp.int32) * T + route[:, 1].astype(jnp.int32)
    cnt = counts[0, :N_CLASSES].astype(jnp.int32)
    tiles_gc = ((cnt + TM_MOE - 1) // TM_MOE).reshape(N_GROUPS, PAIRS_PER_GROUP)
    group_pad = -jnp.sum(tiles_gc, axis=1) % TILES_PER_STEP
    tiles_c = tiles_gc.at[:, -1].add(group_pad).reshape(N_CLASSES)
    tile_end = jnp.cumsum(tiles_c)
    n_tiles = tile_end[-1]
    row_off = jnp.concatenate([(tile_end - tiles_c) * TM_MOE, (n_tiles * TM_MOE)[None],
                               jnp.zeros((LANES - N_CLASSES - 1,), jnp.int32)])
    tile_ids = jnp.arange(n_tiles_max, dtype=jnp.int32)
    tile_cls = jnp.sum(jnp.minimum(tile_ids, n_tiles - 1)[:, None] >= tile_end[None, :], axis=1, dtype=jnp.int32)
    pair_a, pair_b = [], []
    for g in range(N_GROUPS):
        for a in range(EXPERTS_PER_GROUP):
            for b in range(a + 1, EXPERTS_PER_GROUP):
                pair_a.append(g * EXPERTS_PER_GROUP + a)
                pair_b.append(g * EXPERTS_PER_GROUP + b)
    cls_onehot = (tile_cls[:, None] == jnp.arange(N_CLASSES, dtype=jnp.int32)[None, :]).astype(jnp.int32)
    tile_a = cls_onehot @ jnp.array(pair_a, jnp.int32)
    tile_b = cls_onehot @ jnp.array(pair_b, jnp.int32)

    ys8 = _run_moe(key, row_off, tile_a, tile_b, tp8, w_r, b_r,
                   w_gate[l].astype(BF16), w_up[l].astype(BF16), w_down[l].astype(BF16))
    out = _run_final(key, row_off, h2, ys8, final_norm_g[None, :])
    return out.reshape(B, S, D)
```

```python
import functools
import math

import jax
import jax.numpy as jnp
from jax import lax
from jax.experimental import pallas as pl
from jax.experimental.pallas import tpu as pltpu

F32 = jnp.float32
BF16 = jnp.bfloat16

D_MODEL = 1024
MLA_HEADS = 8
MLA_QK_NOPE = 64
MLA_QK_ROPE = 32
MLA_V_DIM = 64
MLA_Q_RANK = 384
MLA_KV_RANK = 256
DIFF_HEADS = 4
DIFF_HALF = 64
DIFF_V_DIM = 2 * DIFF_HALF
N_GROUPS = 4
EXPERTS_PER_GROUP = 8
N_EXPERTS = N_GROUPS * EXPERTS_PER_GROUP
D_FF_EXPERT = 256
ROPE_THETA = 10000.0
EPS = 1e-6
LAM_INIT = 0.8 - 0.6 * math.exp(-0.3 * 0)

LANES = 128
N_QK_BLOCKS = MLA_HEADS + DIFF_HEADS
N_V_BLOCKS = MLA_HEADS // 2 + DIFF_HEADS
N_UNITS = MLA_HEADS // 2 + DIFF_HEADS
PROJ_COLS = MLA_Q_RANK + MLA_KV_RANK + LANES + 3 * DIFF_HEADS * DIFF_V_DIM
LOG2E = 1.4426950408889634
NEG_BIG = -1e30

TM_PROJ = 1024
PROJ_SPLIT = 4
TQ = 256
TK = 256
SCORE_LEAD = 6
KV_UNROLL = 4
TM_POST = 1024
POST_SPLIT = 4
TM_MOE = 128
TM_FIN = 512
SUBLANES = 8
PAIRS_PER_GROUP = EXPERTS_PER_GROUP * (EXPERTS_PER_GROUP - 1) // 2
N_CLASSES = N_GROUPS * PAIRS_PER_GROUP
VMEM_LIMIT = 48 * 1024 * 1024


def _rms(x, g):
    return x * lax.rsqrt(jnp.mean(x * x, axis=-1, keepdims=True) + EPS) * g


def _proj_kernel(x_ref, pos_ref, freq_ref, g_attn_ref, w_in_ref, gq_ref, w_uq_ref, gkv_ref, w_ukv_ref,
                 qa_ref, ka_ref, va_ref):
    tm = x_ref.shape[1] // PROJ_SPLIT
    parts = [slice(part * tm, (part + 1) * tm) for part in range(PROJ_SPLIT)]
    projs = [jnp.dot(_rms(x_ref[0, rows], g_attn_ref[...]).astype(BF16), w_in_ref[...],
                     preferred_element_type=F32) for rows in parts]
    ups = []
    for proj in projs:
        c_q = proj[:, :MLA_Q_RANK]
        c_kv = proj[:, MLA_Q_RANK:MLA_Q_RANK + MLA_KV_RANK]
        ups.append((jnp.dot(_rms(c_q, gq_ref[...]).astype(BF16), w_uq_ref[...], preferred_element_type=F32),
                    jnp.dot(_rms(c_kv, gkv_ref[...]).astype(BF16), w_ukv_ref[...], preferred_element_type=F32)))
    for rows, proj, (q, kv) in zip(parts, projs, ups):
        _rope_and_store(rows, proj, q, kv, pos_ref, freq_ref, qa_ref, ka_ref, va_ref)


def _rope_and_store(rows, proj, q, kv, pos_ref, freq_ref, qa_ref, ka_ref, va_ref):
    ang = pos_ref[0, rows] * freq_ref[...]
    cos_t, sin_t = jnp.cos(ang), jnp.sin(ang)
    lane = lax.broadcasted_iota(jnp.int32, ang.shape, 1)
    in_rope = (lane >= 64) & (lane < 96)
    cos_a = jnp.where(in_rope, cos_t, 1.0)
    sin_a_hi = jnp.where((lane >= 80) & (lane < 96), sin_t, 0.0)
    sin_a_lo = jnp.where((lane >= 64) & (lane < 80), -sin_t, 0.0)
    cos_b = jnp.where(lane < 64, cos_t, pltpu.roll(cos_t, 64, 1))
    sin_b = jnp.where(lane < 64, sin_t, pltpu.roll(sin_t, 64, 1))
    sub = lane % 64
    sin_b_hi = jnp.where(sub >= 32, sin_b, 0.0)
    sin_b_lo = jnp.where(sub < 32, -sin_b, 0.0)

    def rope_a(xb, scale):
        return (xb * (cos_a * scale) + pltpu.roll(xb, 16, 1) * (sin_a_hi * scale)
                + pltpu.roll(xb, 112, 1) * (sin_a_lo * scale))

    def rope_b(xb, scale):
        return (xb * (cos_b * scale) + pltpu.roll(xb, 32, 1) * (sin_b_hi * scale)
                + pltpu.roll(xb, 96, 1) * (sin_b_lo * scale))

    base = MLA_Q_RANK + MLA_KV_RANK
    kpe = rope_a(proj[:, base:base + LANES], 1.0)
    base += LANES

    scale_mla = (MLA_QK_NOPE + MLA_QK_ROPE) ** -0.5 * LOG2E
    scale_diff = DIFF_HALF ** -0.5 * LOG2E
    for h in range(MLA_HEADS):
        sl = slice(h * LANES, (h + 1) * LANES)
        qa_ref[0, h, rows] = rope_a(q[:, sl], scale_mla).astype(BF16)
        ka_ref[0, h, rows] = (kv[:, sl] + kpe).astype(BF16)
    vbase = MLA_HEADS * LANES
    for hp in range(MLA_HEADS // 2):
        va_ref[0, hp, :, rows] = kv[:, vbase + hp * LANES: vbase + (hp + 1) * LANES].T.astype(BF16)
    width = DIFF_HEADS * DIFF_V_DIM
    for h in range(DIFF_HEADS):
        sl = slice(base + h * LANES, base + (h + 1) * LANES)
        qa_ref[0, MLA_HEADS + h, rows] = rope_b(proj[:, sl], scale_diff).astype(BF16)
        sl = slice(base + width + h * LANES, base + width + (h + 1) * LANES)
        ka_ref[0, MLA_HEADS + h, rows] = rope_b(proj[:, sl], 1.0).astype(BF16)
        sl = slice(base + 2 * width + h * LANES, base + 2 * width + (h + 1) * LANES)
        va_ref[0, MLA_HEADS // 2 + h, :, rows] = proj[:, sl].T.astype(BF16)


def _run_proj(x, pos_f, freq, g_attn, w_in_r, gq, w_uq_r, gkv, w_ukv_r):
    B, S, _ = x.shape
    tm = TM_PROJ
    full = lambda shape: pl.BlockSpec(shape, lambda b, i: (0,) * len(shape))
    head_out = lambda n: pl.BlockSpec((1, n, tm, LANES), lambda b, i: (b, 0, i, 0))
    return pl.pallas_call(
        _proj_kernel,
        grid=(B, S // tm),
        in_specs=[
            pl.BlockSpec((1, tm, D_MODEL), lambda b, i: (b, i, 0)),
            pl.BlockSpec((1, tm, 1), lambda b, i: (b, i, 0)),
            full((1, LANES)),
            full((1, D_MODEL)),
            full((D_MODEL, PROJ_COLS)),
            full((1, MLA_Q_RANK)),
            full((MLA_Q_RANK, MLA_HEADS * LANES)),
            full((1, MLA_KV_RANK)),
            full((MLA_KV_RANK, MLA_HEADS * LANES + MLA_HEADS * MLA_V_DIM)),
        ],
        out_specs=[head_out(N_QK_BLOCKS), head_out(N_QK_BLOCKS),
                   pl.BlockSpec((1, N_V_BLOCKS, LANES, tm), lambda b, i: (b, 0, 0, i))],
        out_shape=[
            jax.ShapeDtypeStruct((B, N_QK_BLOCKS, S, LANES), BF16),
            jax.ShapeDtypeStruct((B, N_QK_BLOCKS, S, LANES), BF16),
            jax.ShapeDtypeStruct((B, N_V_BLOCKS, LANES, S), BF16),
        ],
        compiler_params=pltpu.CompilerParams(
            dimension_semantics=("arbitrary", "arbitrary"), vmem_limit_bytes=VMEM_LIMIT),
        name="proj",
    )(x, pos_f, freq, g_attn, w_in_r, gq, w_uq_r, gkv, w_ukv_r)


def _attn_kernel(qa_ref, ka_ref, vt_ref, lam_ref, subln_ref, wg_ref, wu_ref, wd_ref,
                 o_ref, wg16_ref, wu16_ref, wd16_ref, m_ref, l_ref, acc_ref):
    qi = pl.program_id(1)
    m_ref[...] = jnp.full(m_ref.shape, NEG_BIG, F32)
    l_ref[...] = jnp.zeros(l_ref.shape, F32)
    acc_ref[...] = jnp.zeros(acc_ref.shape, F32)

    low_half = lax.broadcasted_iota(jnp.int32, (TQ, LANES), 1) < 64
    is_mla = lambda u: u < MLA_HEADS // 2
    v_rows = lambda u, s: slice(s * MLA_V_DIM, (s + 1) * MLA_V_DIM) if is_mla(u) else slice(0, DIFF_V_DIM)

    def unit_operands(u):
        if is_mla(u):
            return (qa_ref[0, 2 * u], 2 * u), (qa_ref[0, 2 * u + 1], 2 * u + 1), u
        h = u - MLA_HEADS // 2
        q = qa_ref[0, MLA_HEADS + h]
        zero = jnp.zeros_like(q)
        return ((jnp.where(low_half, q, zero), MLA_HEADS + h),
                (jnp.where(low_half, zero, q), MLA_HEADS + h), MLA_HEADS // 2 + h)

    def scores(q, k, mask):
        st = lax.dot_general(k, q, (((1,), (1,)), ((), ())), preferred_element_type=F32)
        return st if mask is None else jnp.where(mask, st, NEG_BIG)

    def softmax_update(slot, rows, st, vt):
        m_prev = m_ref[slot]
        m_new = jnp.maximum(m_prev, jnp.max(st, axis=0, keepdims=True))
        alpha = jnp.exp2(m_prev - m_new)
        pt = jnp.exp2(st - m_new)
        l_ref[slot] = alpha * l_ref[slot] + jnp.sum(pt, axis=0, keepdims=True)
        acc_ref[slot, rows] = alpha * acc_ref[slot, rows] + jnp.dot(vt[rows], pt.astype(BF16),
                                                                    preferred_element_type=F32)
        m_ref[slot] = m_new

    streams = []
    for u in range(N_UNITS):
        (q0, k0), (q1, k1), vi = unit_operands(u)
        streams += [(2 * u, v_rows(u, 0), q0, k0, vi), (2 * u + 1, v_rows(u, 1), q1, k1, vi)]

    def kv_tiles(tiles):
        work = [(j, mask, st) for j, mask in tiles for st in streams]
        pending = {}
        for n in range(len(work) + SCORE_LEAD):
            if n < len(work):
                j, mask, (_, _, q, kidx, _) = work[n]
                keys = pl.ds(pl.multiple_of(j * TK, TK), TK)
                pending[n] = scores(q, ka_ref[0, kidx, keys, :], mask)
            if n >= SCORE_LEAD:
                j, _, (slot, rows, _, _, vi) = work[n - SCORE_LEAD]
                keys = pl.ds(pl.multiple_of(j * TK, TK), TK)
                softmax_update(slot, rows, pending.pop(n - SCORE_LEAD), vt_ref[0, vi, :, keys])

    def body(jj, carry):
        kv_tiles([(KV_UNROLL * jj + d, None) for d in range(KV_UNROLL)])
        return carry

    lax.fori_loop(0, qi // KV_UNROLL, body, 0)
    key_pos = lax.broadcasted_iota(jnp.int32, (TK, TQ), 0)
    query_pos = lax.broadcasted_iota(jnp.int32, (TK, TQ), 1)
    causal = key_pos <= query_pos
    for rest in range(KV_UNROLL):
        @pl.when(qi % KV_UNROLL == rest)
        def _():
            for w_ref, w16_ref in ((wg_ref, wg16_ref), (wu_ref, wu16_ref), (wd_ref, wd16_ref)):
                w16_ref[...] = w_ref[...].astype(BF16)
            kv_tiles([(qi - rest + d, None) for d in range(rest)] + [(qi, causal)])

    lam_v = lam_ref[...]
    lam = (jnp.exp(jnp.sum(lam_v[0:1] * lam_v[1:2], axis=1, keepdims=True))
           - jnp.exp(jnp.sum(lam_v[2:3] * lam_v[3:4], axis=1, keepdims=True)) + LAM_INIT)
    for u in range(N_UNITS):
        r0, r1 = v_rows(u, 0), v_rows(u, 1)
        o0 = acc_ref[2 * u, r0] / l_ref[2 * u]
        o1 = acc_ref[2 * u + 1, r1] / l_ref[2 * u + 1]
        if is_mla(u):
            out = jnp.concatenate([o0, o1], axis=0).T
            col0 = u * LANES
        else:
            d = o0 - lam * o1
            d = d * lax.rsqrt(jnp.mean(d * d, axis=0, keepdims=True) + EPS)
            out = d.T * subln_ref[...] * (1.0 - LAM_INIT)
            col0 = MLA_HEADS * MLA_V_DIM + (u - MLA_HEADS // 2) * LANES
        o_ref[0, :, col0:col0 + LANES] = out.astype(BF16)


def _run_attn(qa, ka, vt, lam_vecs, subln, expert_weights):
    B, _, S, _ = qa.shape
    n_q = S // TQ
    n_slots = 2 * N_UNITS
    flat = [w.reshape(-1, w.shape[-1]) for w in expert_weights]
    slab = lambda w: pl.BlockSpec((w.shape[0] // (B * n_q), w.shape[1]), lambda b, i: (b * n_q + i, 0))
    assert all(w.shape[0] % (B * n_q * 16) == 0 for w in flat)
    outs = pl.pallas_call(
        _attn_kernel,
        grid=(B, n_q),
        in_specs=[
            pl.BlockSpec((1, N_QK_BLOCKS, TQ, LANES), lambda b, i: (b, 0, i, 0)),
            pl.BlockSpec((1, N_QK_BLOCKS, S, LANES), lambda b, i: (b, 0, 0, 0)),
            pl.BlockSpec((1, N_V_BLOCKS, LANES, S), lambda b, i: (b, 0, 0, 0)),
            pl.BlockSpec((4, DIFF_HALF), lambda b, i: (0, 0)),
            pl.BlockSpec((1, DIFF_V_DIM), lambda b, i: (0, 0)),
        ] + [slab(w) for w in flat],
        out_specs=[pl.BlockSpec((1, TQ, D_MODEL), lambda b, i: (b, i, 0))] + [slab(w) for w in flat],
        out_shape=[jax.ShapeDtypeStruct((B, S, D_MODEL), BF16)]
                  + [jax.ShapeDtypeStruct(w.shape, BF16) for w in flat],
        scratch_shapes=[
            pltpu.VMEM((n_slots, 1, TQ), F32),
            pltpu.VMEM((n_slots, 1, TQ), F32),
            pltpu.VMEM((n_slots, LANES, TQ), F32),
        ],
        compiler_params=pltpu.CompilerParams(
            dimension_semantics=("arbitrary", "arbitrary"), vmem_limit_bytes=VMEM_LIMIT),
        name="attn",
    )(qa, ka, vt, lam_vecs, subln, *flat)
    return outs[0], [w16.reshape(w.shape) for w16, w in zip(outs[1:], expert_weights)]


def _post_kernel(o_ref, x_ref, w_o_ref, g_ffn_ref, w_r_ref, b_r_ref, tri_ref,
                 h_ref, tp_ref, route_ref, cnt_ref):
    tm = x_ref.shape[0]

    @pl.when(pl.program_id(0) == 0)
    def _():
        cnt_ref[...] = jnp.zeros(cnt_ref.shape, F32)

    tp_rows = tm // POST_SPLIT
    parts = [slice(p * tp_rows, (p + 1) * tp_rows) for p in range(POST_SPLIT)]
    ts = []
    for rows in parts:
        h = x_ref[rows] + jnp.dot(o_ref[rows], w_o_ref[...], preferred_element_type=F32)
        h_ref[rows] = h
        ts.append(_rms(h, g_ffn_ref[...]))
    logits = [jnp.dot(t.astype(BF16), w_r_ref[...], preferred_element_type=F32) + b_r_ref[...] for t in ts]
    routed = [_route_class(lg) for lg in logits]

    seen = cnt_ref[...]
    for rows, (cls, onehot) in zip(parts, routed):
        before = jnp.dot(tri_ref[...], onehot.astype(BF16), preferred_element_type=F32) + seen
        rank = jnp.sum(before * onehot, axis=1, keepdims=True)
        seen = seen + jnp.sum(onehot, axis=0, keepdims=True)
        lane = lax.broadcasted_iota(jnp.int32, onehot.shape, 1)
        route_ref[rows] = jnp.where(lane == 0, cls, jnp.where(lane == 1, rank, 0.0))
    cnt_ref[...] = seen

    for rows, t in zip(parts, ts):
        for c in range(SUBLANES):
            tp_ref[pl.ds(rows.start * SUBLANES + c, tp_rows, stride=SUBLANES), :] = t[:, c * LANES:(c + 1) * LANES]


def _route_class(lg):
    lane = lax.broadcasted_iota(jnp.int32, lg.shape, 1)
    lane_f = lane.astype(F32)
    first_lane = lambda hit: jnp.min(jnp.where(hit, lane_f, float(LANES)), axis=1, keepdims=True)

    is_group = lane < N_GROUPS
    gl = jnp.where(is_group, lg, NEG_BIG)
    gmax = jnp.max(gl, axis=1, keepdims=True)
    gidx = first_lane(gl == gmax)

    lo = N_GROUPS + EXPERTS_PER_GROUP * gidx
    el = jnp.where((lane_f >= lo) & (lane_f < lo + EXPERTS_PER_GROUP), lg, NEG_BIG)
    m1 = jnp.max(el, axis=1, keepdims=True)
    i1 = first_lane(el == m1)
    el2 = jnp.where(lane_f == i1, NEG_BIG, el)
    m2 = jnp.max(el2, axis=1, keepdims=True)
    i2 = first_lane(el2 == m2)

    a = jnp.minimum(i1, i2) - lo
    b = jnp.maximum(i1, i2) - lo
    pair = a * (2 * EXPERTS_PER_GROUP - 1 - a) * 0.5 + (b - a - 1.0)
    cls = gidx * PAIRS_PER_GROUP + pair
    return cls, jnp.where(lane_f == cls, 1.0, 0.0)


def _run_post(o2, x2, w_o, g_ffn, w_r, b_r, tri):
    T = x2.shape[0]
    tm = TM_POST
    full = lambda shape: pl.BlockSpec(shape, lambda i: (0,) * len(shape))
    rows = lambda w: pl.BlockSpec((tm, w), lambda i: (i, 0))
    return pl.pallas_call(
        _post_kernel,
        grid=(T // tm,),
        in_specs=[rows(D_MODEL), rows(D_MODEL), full((D_MODEL, D_MODEL)), full((1, D_MODEL)),
                  full((D_MODEL, LANES)), full((1, LANES)), full((tm // POST_SPLIT, tm // POST_SPLIT))],
        out_specs=[rows(D_MODEL), pl.BlockSpec((tm * SUBLANES, LANES), lambda i: (i, 0)), rows(LANES),
                   full((1, LANES))],
        out_shape=[
            jax.ShapeDtypeStruct((T, D_MODEL), F32),
            jax.ShapeDtypeStruct((T * SUBLANES, LANES), F32),
            jax.ShapeDtypeStruct((T, LANES), F32),
            jax.ShapeDtypeStruct((1, LANES), F32),
        ],
        compiler_params=pltpu.CompilerParams(
            dimension_semantics=("arbitrary",), vmem_limit_bytes=VMEM_LIMIT),
        name="post",
    )(o2, x2, w_o, g_ffn, w_r, b_r, tri)


TILE_ROWS = TM_MOE * SUBLANES


TILES_PER_STEP = 4
GATHER_AHEAD = 2
GATHER_SLOTS = GATHER_AHEAD + 1


def _sorted_row(key_ref, off_ref, t):
    n_tok = key_ref.shape[0]
    key = key_ref[t]
    return off_ref[key >> (n_tok.bit_length() - 1)] + (key & (n_tok - 1))


def _moe_kernel(key_ref, off_ref, ta_ref, tb_ref, tp_hbm, w_r_ref, b_r_ref, wg_ref, wu_ref, wd_ref,
                y_ref, src_ref, xbuf, sem):
    step = pl.program_id(0)
    n_tok = key_ref.shape[0]
    n_used = off_ref[N_CLASSES] // TM_MOE

    def start_gather(tile):
        slot = tile % GATHER_SLOTS
        for r in range(TM_MOE):
            tok = src_ref[tile * TM_MOE + r]
            pltpu.make_async_copy(
                tp_hbm.at[pl.ds(pl.multiple_of(tok * SUBLANES, SUBLANES), SUBLANES), :],
                xbuf.at[pl.ds(pl.multiple_of(slot * TILE_ROWS + r * SUBLANES, SUBLANES), SUBLANES), :],
                sem.at[slot]).start(priority=r % 2)

    def wait_gather(tile):
        slot = tile % GATHER_SLOTS
        pltpu.make_async_copy(
            tp_hbm.at[pl.ds(0, TILE_ROWS), :],
            xbuf.at[pl.ds(pl.multiple_of(slot * TILE_ROWS, SUBLANES), TILE_ROWS), :],
            sem.at[slot]).wait()

    @pl.when(step == 0)
    def _():
        def fill(tile, c):
            for r in range(TM_MOE):
                src_ref[tile * TM_MOE + r] = (tile * TM_MOE + r) & (n_tok - 1)
            return c

        def invert(t, c):
            src_ref[_sorted_row(key_ref, off_ref, t)] = t
            return c

        lax.fori_loop(0, n_used + GATHER_AHEAD, fill, 0)
        lax.fori_loop(0, n_tok, invert, 0, unroll=16)
        for tile in range(GATHER_AHEAD):
            start_gather(tile)

    for sub in range(TILES_PER_STEP):
        _moe_tile(step * TILES_PER_STEP + sub, sub, n_used, ta_ref, tb_ref, w_r_ref, b_r_ref,
                  wg_ref, wu_ref, wd_ref, y_ref, xbuf, start_gather, wait_gather)


def _moe_tile(i, sub, n_used, ta_ref, tb_ref, w_r_ref, b_r_ref, wg_ref, wu_ref, wd_ref, y_ref, xbuf,
              start_gather, wait_gather):
    out0 = sub * TILE_ROWS

    @pl.when(i < n_used)
    def _():
        wait_gather(i)
        base = (i % GATHER_SLOTS) * TILE_ROWS
        x = jnp.concatenate([xbuf[pl.ds(base + c, TM_MOE, stride=SUBLANES), :] for c in range(SUBLANES)],
                            axis=1).astype(BF16)

        lg = jnp.dot(x, w_r_ref[...], preferred_element_type=F32) + b_r_ref[...]
        lane = lax.broadcasted_iota(jnp.int32, lg.shape, 1)
        pick = lambda idx: jnp.sum(jnp.where(lane == idx, lg, 0.0), axis=1, keepdims=True)
        ea, eb = ta_ref[i], tb_ref[i]
        g_logit = pick(ea // EXPERTS_PER_GROUP)
        p_group = 1.0 / jnp.sum(jnp.where(lane < N_GROUPS, jnp.exp(lg - g_logit), 0.0), axis=1, keepdims=True)
        l_a, l_b = pick(N_GROUPS + ea), pick(N_GROUPS + eb)
        gate_a = p_group / (1.0 + jnp.exp(l_b - l_a))
        gate_b = p_group / (1.0 + jnp.exp(l_a - l_b))

        def hidden(e_loc, gate):
            a = jnp.dot(x, wg_ref[e_loc], preferred_element_type=F32)
            u = jnp.dot(x, wu_ref[e_loc], preferred_element_type=F32)
            return ((a * jax.nn.sigmoid(a)) * u * gate).astype(BF16)

        a_loc, b_loc = ea % EXPERTS_PER_GROUP, eb % EXPERTS_PER_GROUP
        y = (jnp.dot(hidden(a_loc, gate_a), wd_ref[a_loc], preferred_element_type=F32)
             + jnp.dot(hidden(b_loc, gate_b), wd_ref[b_loc], preferred_element_type=F32))
        for c in range(SUBLANES):
            y_ref[pl.ds(out0 + c, TM_MOE, stride=SUBLANES), :] = y[:, c * LANES:(c + 1) * LANES]
        start_gather(i + GATHER_AHEAD)

    @pl.when(i >= n_used)
    def _():
        y_ref[pl.ds(out0, TILE_ROWS), :] = jnp.zeros((TILE_ROWS, LANES), F32)

        @pl.when(i < n_used + GATHER_AHEAD)
        def _():
            wait_gather(i)


def _run_moe(key, row_off, tile_a, tile_b, tp8, w_r, b_r, wg, wu, wd):
    n_tiles = tile_a.shape[0]
    n_tok = key.shape[0]
    assert n_tok & (n_tok - 1) == 0, "token count must be a power of two (class/rank key packing)"
    assert n_tiles % TILES_PER_STEP == 0
    group = lambda i, k, o, ta, tb: (ta[i * TILES_PER_STEP] // EXPERTS_PER_GROUP, 0, 0)
    grid_spec = pltpu.PrefetchScalarGridSpec(
        num_scalar_prefetch=4,
        grid=(n_tiles // TILES_PER_STEP,),
        in_specs=[
            pl.BlockSpec(memory_space=pl.ANY),
            pl.BlockSpec((D_MODEL, LANES), lambda i, k, o, ta, tb: (0, 0)),
            pl.BlockSpec((1, LANES), lambda i, k, o, ta, tb: (0, 0)),
            pl.BlockSpec((EXPERTS_PER_GROUP, D_MODEL, D_FF_EXPERT), group),
            pl.BlockSpec((EXPERTS_PER_GROUP, D_MODEL, D_FF_EXPERT), group),
            pl.BlockSpec((EXPERTS_PER_GROUP, D_FF_EXPERT, D_MODEL), group),
        ],
        out_specs=pl.BlockSpec((TILES_PER_STEP * TILE_ROWS, LANES), lambda i, k, o, ta, tb: (i, 0)),
        scratch_shapes=[
            pltpu.SMEM((n_tiles * TM_MOE,), jnp.int32),
            pltpu.VMEM((GATHER_SLOTS * TILE_ROWS, LANES), F32),
            pltpu.SemaphoreType.DMA((GATHER_SLOTS,)),
        ],
    )
    return pl.pallas_call(
        _moe_kernel,
        grid_spec=grid_spec,
        out_shape=jax.ShapeDtypeStruct((n_tiles * TILE_ROWS, LANES), F32),
        compiler_params=pltpu.CompilerParams(
            dimension_semantics=("arbitrary",), vmem_limit_bytes=VMEM_LIMIT),
        name="moe",
    )(key, row_off, tile_a, tile_b, tp8, w_r, b_r, wg, wu, wd)


FIN_ROWS = TM_FIN * SUBLANES


def _final_kernel(key_ref, off_ref, h_ref, y_hbm, g_ref, out_ref, ybuf, sem):
    i = pl.program_id(0)
    slot = i % 2

    def start_gather(tile, slot):
        def body(r2, c):
            for q in range(2):
                r = 2 * r2 + q
                p = _sorted_row(key_ref, off_ref, tile * TM_FIN + r)
                pltpu.make_async_copy(
                    y_hbm.at[pl.ds(pl.multiple_of(p * SUBLANES, SUBLANES), SUBLANES), :],
                    ybuf.at[pl.ds(pl.multiple_of(slot * FIN_ROWS + r * SUBLANES, SUBLANES), SUBLANES), :],
                    sem.at[slot]).start(priority=q)
            return c
        lax.fori_loop(0, TM_FIN // 2, body, 0, unroll=4)

    @pl.when(i == 0)
    def _():
        start_gather(0, 0)

    @pl.when(i + 1 < pl.num_programs(0))
    def _():
        start_gather(i + 1, 1 - slot)

    pltpu.make_async_copy(
        y_hbm.at[pl.ds(0, FIN_ROWS), :],
        ybuf.at[pl.ds(pl.multiple_of(slot * FIN_ROWS, SUBLANES), FIN_ROWS), :],
        sem.at[slot]).wait()
    base = slot * FIN_ROWS
    y = jnp.concatenate([ybuf[pl.ds(base + c, TM_FIN, stride=SUBLANES), :] for c in range(SUBLANES)], axis=1)
    out_ref[...] = _rms(h_ref[...] + y, g_ref[...])


def _run_final(key, row_off, h2, ys8, g_final):
    T = h2.shape[0]
    grid_spec = pltpu.PrefetchScalarGridSpec(
        num_scalar_prefetch=2,
        grid=(T // TM_FIN,),
        in_specs=[
            pl.BlockSpec((TM_FIN, D_MODEL), lambda i, k, o: (i, 0)),
            pl.BlockSpec(memory_space=pl.ANY),
            pl.BlockSpec((1, D_MODEL), lambda i, k, o: (0, 0)),
        ],
        out_specs=pl.BlockSpec((TM_FIN, D_MODEL), lambda i, k, o: (i, 0)),
        scratch_shapes=[pltpu.VMEM((2 * FIN_ROWS, LANES), F32), pltpu.SemaphoreType.DMA((2,))],
    )
    return pl.pallas_call(
        _final_kernel,
        grid_spec=grid_spec,
        out_shape=jax.ShapeDtypeStruct((T, D_MODEL), F32),
        compiler_params=pltpu.CompilerParams(
            dimension_semantics=("arbitrary",), vmem_limit_bytes=VMEM_LIMIT),
        name="final",
    )(key, row_off, h2, ys8, g_final)


def _prep_attention_weights(w_in, w_uq, w_ukv):
    d = w_in.shape[0]
    c0 = MLA_Q_RANK + MLA_KV_RANK
    kpe = w_in[:, c0:c0 + MLA_QK_ROPE]
    kpe_blk = jnp.concatenate(
        [jnp.zeros((d, MLA_QK_NOPE), F32), kpe, jnp.zeros((d, LANES - MLA_QK_NOPE - MLA_QK_ROPE), F32)], axis=1)
    w_in_r = jnp.concatenate([w_in[:, :c0], kpe_blk, w_in[:, c0 + MLA_QK_ROPE:]], axis=1).astype(BF16)

    dqk = MLA_QK_NOPE + MLA_QK_ROPE
    uq = w_uq.reshape(MLA_Q_RANK, MLA_HEADS, dqk)
    uq = jnp.pad(uq, ((0, 0), (0, 0), (0, LANES - dqk))).reshape(MLA_Q_RANK, MLA_HEADS * LANES).astype(BF16)

    ukv = w_ukv.reshape(MLA_KV_RANK, MLA_HEADS, MLA_QK_NOPE + MLA_V_DIM)
    uk = jnp.pad(ukv[:, :, :MLA_QK_NOPE], ((0, 0), (0, 0), (0, LANES - MLA_QK_NOPE)))
    uk = uk.reshape(MLA_KV_RANK, MLA_HEADS * LANES)
    uv = ukv[:, :, MLA_QK_NOPE:].reshape(MLA_KV_RANK, MLA_HEADS * MLA_V_DIM)
    w_ukv_r = jnp.concatenate([uk, uv], axis=1).astype(BF16)
    return w_in_r, uq, w_ukv_r


def _rope_freq_lanes():
    fb = ROPE_THETA ** (-jnp.arange(DIFF_HALF // 2, dtype=F32) / (DIFF_HALF // 2))
    fa = ROPE_THETA ** (-jnp.arange(MLA_QK_ROPE // 2, dtype=F32) / (MLA_QK_ROPE // 2))
    return jnp.concatenate([fb, fb, fa, fa, jnp.zeros((LANES - 96,), F32)])[None, :]


def kernel(x, positions, attn_norm_g, w_in, q_norm_g, w_uq, kv_norm_g, w_ukv, lambda_q1, lambda_k1, lambda_q2, lambda_k2, subln_g, w_o, ffn_norm_g, w_router_group, b_router_group, w_router_expert, b_router_expert, w_gate, w_up, w_down, final_norm_g):
    B, S, D = x.shape
    T = B * S
    l = 0

    w_in_r, w_uq_r, w_ukv_r = _prep_attention_weights(w_in[l], w_uq[l], w_ukv[l])
    pos_f = positions.astype(F32)[:, :, None]
    qa, ka, va = _run_proj(x, pos_f, _rope_freq_lanes(), attn_norm_g[l][None, :], w_in_r,
                           q_norm_g[l][None, :], w_uq_r, kv_norm_g[l][None, :], w_ukv_r)

    lam_vecs = jnp.stack([lambda_q1[l], lambda_k1[l], lambda_q2[l], lambda_k2[l]]).astype(F32)
    o, (wg16, wu16, wd16) = _run_attn(qa, ka, va, lam_vecs, subln_g[l][None, :],
                                      (w_gate[l], w_up[l], w_down[l]))

    w_r = jnp.concatenate(
        [w_router_group[l], w_router_expert[l].reshape(D, N_EXPERTS),
         jnp.zeros((D, LANES - N_GROUPS - N_EXPERTS), F32)], axis=1).astype(BF16)
    b_r = jnp.concatenate(
        [b_router_group[l], b_router_expert[l].reshape(N_EXPERTS),
         jnp.zeros((LANES - N_GROUPS - N_EXPERTS,), F32)])[None, :].astype(F32)
    ii = jnp.arange(TM_POST // POST_SPLIT)
    tri = (ii[None, :] < ii[:, None]).astype(BF16)
    x2 = x.reshape(T, D)
    h2, tp8, route, counts = _run_post(o.reshape(T, D), x2, w_o[l].astype(BF16), ffn_norm_g[l][None, :],
                                       w_r, b_r, tri)

    pad_tiles = N_GROUPS * (TILES_PER_STEP - 1)
    n_tiles_max = T // TM_MOE + N_CLASSES - 1 + pad_tiles + GATHER_AHEAD
    n_tiles_max += -n_tiles_max % TILES_PER_STEP
    key = route[:, 0].astype(jnp.int32) * T + route[:, 1].astype(jnp.int32)
    cnt = counts[0, :N_CLASSES].astype(jnp.int32)
    tiles_gc = ((cnt + TM_MOE - 1) // TM_MOE).reshape(N_GROUPS, PAIRS_PER_GROUP)
    group_pad = -jnp.sum(tiles_gc, axis=1) % TILES_PER_STEP
    tiles_c = tiles_gc.at[:, -1].add(group_pad).reshape(N_CLASSES)
    tile_end = jnp.cumsum(tiles_c)
    n_tiles = tile_end[-1]
    row_off = jnp.concatenate([(tile_end - tiles_c) * TM_MOE, (n_tiles * TM_MOE)[None],
                               jnp.zeros((LANES - N_CLASSES - 1,), jnp.int32)])
    tile_ids = jnp.arange(n_tiles_max, dtype=jnp.int32)
    tile_cls = jnp.sum(jnp.minimum(tile_ids, n_tiles - 1)[:, None] >= tile_end[None, :], axis=1, dtype=jnp.int32)
    pair_a, pair_b = [], []
    for g in range(N_GROUPS):
        for a in range(EXPERTS_PER_GROUP):
            for b in range(a + 1, EXPERTS_PER_GROUP):
                pair_a.append(g * EXPERTS_PER_GROUP + a)
                pair_b.append(g * EXPERTS_PER_GROUP + b)
    cls_onehot = (tile_cls[:, None] == jnp.arange(N_CLASSES, dtype=jnp.int32)[None, :]).astype(jnp.int32)
    tile_a = cls_onehot @ jnp.array(pair_a, jnp.int32)
    tile_b = cls_onehot @ jnp.array(pair_b, jnp.int32)

    ys8 = _run_moe(key, row_off, tile_a, tile_b, tp8, w_r, b_r, wg16, wu16, wd16)
    out = _run_final(key, row_off, h2, ys8, final_norm_g[None, :])
    return out.reshape(B, S, D)
```

```python
import functools
import math

import jax
import jax.numpy as jnp
from jax import lax
from jax.experimental import pallas as pl
from jax.experimental.pallas import tpu as pltpu

F32 = jnp.float32
BF16 = jnp.bfloat16

D_MODEL = 1024
MLA_HEADS = 8
MLA_QK_NOPE = 64
MLA_QK_ROPE = 32
MLA_V_DIM = 64
MLA_Q_RANK = 384
MLA_KV_RANK = 256
DIFF_HEADS = 4
DIFF_HALF = 64
DIFF_V_DIM = 2 * DIFF_HALF
N_GROUPS = 4
EXPERTS_PER_GROUP = 8
N_EXPERTS = N_GROUPS * EXPERTS_PER_GROUP
D_FF_EXPERT = 256
ROPE_THETA = 10000.0
EPS = 1e-6
LAM_INIT = 0.8 - 0.6 * math.exp(-0.3 * 0)

LANES = 128
N_QK_BLOCKS = MLA_HEADS + DIFF_HEADS
N_V_BLOCKS = MLA_HEADS // 2 + DIFF_HEADS
N_UNITS = MLA_HEADS // 2 + DIFF_HEADS
PROJ_COLS = MLA_Q_RANK + MLA_KV_RANK + LANES + 3 * DIFF_HEADS * DIFF_V_DIM
LOG2E = 1.4426950408889634
NEG_BIG = -1e30

TM_PROJ = 1024
PROJ_SPLIT = 4
TQ = 256
TK = 256
SCORE_LEAD = 6
KV_UNROLL = 4
TM_POST = 1024
POST_SPLIT = 4
TM_MOE = 128
TM_FIN = 512
SUBLANES = 8
PAIRS_PER_GROUP = EXPERTS_PER_GROUP * (EXPERTS_PER_GROUP - 1) // 2
N_CLASSES = N_GROUPS * PAIRS_PER_GROUP
VMEM_LIMIT = 48 * 1024 * 1024


def _rms(x, g):
    return x * lax.rsqrt(jnp.mean(x * x, axis=-1, keepdims=True) + EPS) * g


def _proj_kernel(x_ref, pos_ref, freq_ref, g_attn_ref, w_in_ref, gq_ref, w_uq_ref, gkv_ref, w_ukv_ref,
                 qa_ref, ka_ref, va_ref):
    tm = x_ref.shape[1] // PROJ_SPLIT
    parts = [slice(part * tm, (part + 1) * tm) for part in range(PROJ_SPLIT)]
    projs = [jnp.dot(_rms(x_ref[0, rows], g_attn_ref[...]).astype(BF16), w_in_ref[...],
                     preferred_element_type=F32) for rows in parts]
    ups = []
    for proj in projs:
        c_q = proj[:, :MLA_Q_RANK]
        c_kv = proj[:, MLA_Q_RANK:MLA_Q_RANK + MLA_KV_RANK]
        ups.append((jnp.dot(_rms(c_q, gq_ref[...]).astype(BF16), w_uq_ref[...], preferred_element_type=F32),
                    jnp.dot(_rms(c_kv, gkv_ref[...]).astype(BF16), w_ukv_ref[...], preferred_element_type=F32)))
    for rows, proj, (q, kv) in zip(parts, projs, ups):
        _rope_and_store(rows, proj, q, kv, pos_ref, freq_ref, qa_ref, ka_ref, va_ref)


def _rope_and_store(rows, proj, q, kv, pos_ref, freq_ref, qa_ref, ka_ref, va_ref):
    ang = pos_ref[0, rows] * freq_ref[...]
    cos_t, sin_t = jnp.cos(ang), jnp.sin(ang)
    lane = lax.broadcasted_iota(jnp.int32, ang.shape, 1)
    in_rope = (lane >= 64) & (lane < 96)
    cos_a = jnp.where(in_rope, cos_t, 1.0)
    sin_a_hi = jnp.where((lane >= 80) & (lane < 96), sin_t, 0.0)
    sin_a_lo = jnp.where((lane >= 64) & (lane < 80), -sin_t, 0.0)
    cos_b = jnp.where(lane < 64, cos_t, pltpu.roll(cos_t, 64, 1))
    sin_b = jnp.where(lane < 64, sin_t, pltpu.roll(sin_t, 64, 1))
    sub = lane % 64
    sin_b_hi = jnp.where(sub >= 32, sin_b, 0.0)
    sin_b_lo = jnp.where(sub < 32, -sin_b, 0.0)

    def rope_a(xb, scale):
        return (xb * (cos_a * scale) + pltpu.roll(xb, 16, 1) * (sin_a_hi * scale)
                + pltpu.roll(xb, 112, 1) * (sin_a_lo * scale))

    def rope_b(xb, scale):
        return (xb * (cos_b * scale) + pltpu.roll(xb, 32, 1) * (sin_b_hi * scale)
                + pltpu.roll(xb, 96, 1) * (sin_b_lo * scale))

    base = MLA_Q_RANK + MLA_KV_RANK
    kpe = rope_a(proj[:, base:base + LANES], 1.0)
    base += LANES

    scale_mla = (MLA_QK_NOPE + MLA_QK_ROPE) ** -0.5 * LOG2E
    scale_diff = DIFF_HALF ** -0.5 * LOG2E
    for h in range(MLA_HEADS):
        sl = slice(h * LANES, (h + 1) * LANES)
        qa_ref[0, h, rows] = rope_a(q[:, sl], scale_mla).astype(BF16)
        ka_ref[0, h, rows] = (kv[:, sl] + kpe).astype(BF16)
    vbase = MLA_HEADS * LANES
    for hp in range(MLA_HEADS // 2):
        va_ref[0, hp, :, rows] = kv[:, vbase + hp * LANES: vbase + (hp + 1) * LANES].T.astype(BF16)
    width = DIFF_HEADS * DIFF_V_DIM
    for h in range(DIFF_HEADS):
        sl = slice(base + h * LANES, base + (h + 1) * LANES)
        qa_ref[0, MLA_HEADS + h, rows] = rope_b(proj[:, sl], scale_diff).astype(BF16)
        sl = slice(base + width + h * LANES, base + width + (h + 1) * LANES)
        ka_ref[0, MLA_HEADS + h, rows] = rope_b(proj[:, sl], 1.0).astype(BF16)
        sl = slice(base + 2 * width + h * LANES, base + 2 * width + (h + 1) * LANES)
        va_ref[0, MLA_HEADS // 2 + h, :, rows] = proj[:, sl].T.astype(BF16)


def _run_proj(x, pos_f, freq, g_attn, w_in_r, gq, w_uq_r, gkv, w_ukv_r):
    B, S, _ = x.shape
    tm = TM_PROJ
    full = lambda shape: pl.BlockSpec(shape, lambda b, i: (0,) * len(shape))
    head_out = lambda n: pl.BlockSpec((1, n, tm, LANES), lambda b, i: (b, 0, i, 0))
    return pl.pallas_call(
        _proj_kernel,
        grid=(B, S // tm),
        in_specs=[
            pl.BlockSpec((1, tm, D_MODEL), lambda b, i: (b, i, 0)),
            pl.BlockSpec((1, tm, 1), lambda b, i: (b, i, 0)),
            full((1, LANES)),
            full((1, D_MODEL)),
            full((D_MODEL, PROJ_COLS)),
            full((1, MLA_Q_RANK)),
            full((MLA_Q_RANK, MLA_HEADS * LANES)),
            full((1, MLA_KV_RANK)),
            full((MLA_KV_RANK, MLA_HEADS * LANES + MLA_HEADS * MLA_V_DIM)),
        ],
        out_specs=[head_out(N_QK_BLOCKS), head_out(N_QK_BLOCKS),
                   pl.BlockSpec((1, N_V_BLOCKS, LANES, tm), lambda b, i: (b, 0, 0, i))],
        out_shape=[
            jax.ShapeDtypeStruct((B, N_QK_BLOCKS, S, LANES), BF16),
            jax.ShapeDtypeStruct((B, N_QK_BLOCKS, S, LANES), BF16),
            jax.ShapeDtypeStruct((B, N_V_BLOCKS, LANES, S), BF16),
        ],
        compiler_params=pltpu.CompilerParams(
            dimension_semantics=("arbitrary", "arbitrary"), vmem_limit_bytes=VMEM_LIMIT),
        name="proj",
    )(x, pos_f, freq, g_attn, w_in_r, gq, w_uq_r, gkv, w_ukv_r)


def _attn_kernel(qa_ref, ka_ref, vt_ref, lam_ref, subln_ref, wg_ref, wu_ref, wd_ref,
                 o_ref, wg16_ref, wu16_ref, wd16_ref, m_ref, l_ref, acc_ref):
    qi = pl.program_id(1)
    m_ref[...] = jnp.full(m_ref.shape, NEG_BIG, F32)
    l_ref[...] = jnp.zeros(l_ref.shape, F32)
    acc_ref[...] = jnp.zeros(acc_ref.shape, F32)

    low_half = lax.broadcasted_iota(jnp.int32, (TQ, LANES), 1) < 64
    is_mla = lambda u: u < MLA_HEADS // 2
    v_rows = lambda u, s: slice(s * MLA_V_DIM, (s + 1) * MLA_V_DIM) if is_mla(u) else slice(0, DIFF_V_DIM)

    def unit_operands(u):
        if is_mla(u):
            return (qa_ref[0, 2 * u], 2 * u), (qa_ref[0, 2 * u + 1], 2 * u + 1), u
        h = u - MLA_HEADS // 2
        q = qa_ref[0, MLA_HEADS + h]
        zero = jnp.zeros_like(q)
        return ((jnp.where(low_half, q, zero), MLA_HEADS + h),
                (jnp.where(low_half, zero, q), MLA_HEADS + h), MLA_HEADS // 2 + h)

    def scores(q, k, mask):
        st = lax.dot_general(k, q, (((1,), (1,)), ((), ())), preferred_element_type=F32)
        return st if mask is None else jnp.where(mask, st, NEG_BIG)

    def softmax_update(slot, rows, st, vt):
        m_prev = m_ref[slot]
        m_new = jnp.maximum(m_prev, jnp.max(st, axis=0, keepdims=True))
        alpha = jnp.exp2(m_prev - m_new)
        pt = jnp.exp2(st - m_new)
        l_ref[slot] = alpha * l_ref[slot] + jnp.sum(pt, axis=0, keepdims=True)
        acc_ref[slot, rows] = alpha * acc_ref[slot, rows] + jnp.dot(vt[rows], pt.astype(BF16),
                                                                    preferred_element_type=F32)
        m_ref[slot] = m_new

    streams = []
    for u in range(N_UNITS):
        (q0, k0), (q1, k1), vi = unit_operands(u)
        streams += [(2 * u, v_rows(u, 0), q0, k0, vi), (2 * u + 1, v_rows(u, 1), q1, k1, vi)]

    def kv_tiles(tiles):
        work = [(j, mask, st) for j, mask in tiles for st in streams]
        pending = {}
        for n in range(len(work) + SCORE_LEAD):
            if n < len(work):
                j, mask, (_, _, q, kidx, _) = work[n]
                keys = pl.ds(pl.multiple_of(j * TK, TK), TK)
                pending[n] = scores(q, ka_ref[0, kidx, keys, :], mask)
            if n >= SCORE_LEAD:
                j, _, (slot, rows, _, _, vi) = work[n - SCORE_LEAD]
                keys = pl.ds(pl.multiple_of(j * TK, TK), TK)
                softmax_update(slot, rows, pending.pop(n - SCORE_LEAD), vt_ref[0, vi, :, keys])

    def body(jj, carry):
        kv_tiles([(KV_UNROLL * jj + d, None) for d in range(KV_UNROLL)])
        return carry

    lax.fori_loop(0, qi // KV_UNROLL, body, 0)
    key_pos = lax.broadcasted_iota(jnp.int32, (TK, TQ), 0)
    query_pos = lax.broadcasted_iota(jnp.int32, (TK, TQ), 1)
    causal = key_pos <= query_pos
    for rest in range(KV_UNROLL):
        @pl.when(qi % KV_UNROLL == rest)
        def _():
            for w_ref, w16_ref in ((wg_ref, wg16_ref), (wu_ref, wu16_ref), (wd_ref, wd16_ref)):
                w16_ref[...] = w_ref[...].astype(BF16)
            kv_tiles([(qi - rest + d, None) for d in range(rest)] + [(qi, causal)])

    lam_v = lam_ref[...]
    lam = (jnp.exp(jnp.sum(lam_v[0:1] * lam_v[1:2], axis=1, keepdims=True))
           - jnp.exp(jnp.sum(lam_v[2:3] * lam_v[3:4], axis=1, keepdims=True)) + LAM_INIT)
    for u in range(N_UNITS):
        r0, r1 = v_rows(u, 0), v_rows(u, 1)
        o0 = acc_ref[2 * u, r0] / l_ref[2 * u]
        o1 = acc_ref[2 * u + 1, r1] / l_ref[2 * u + 1]
        if is_mla(u):
            out = jnp.concatenate([o0, o1], axis=0).T
            col0 = u * LANES
        else:
            d = o0 - lam * o1
            d = d * lax.rsqrt(jnp.mean(d * d, axis=0, keepdims=True) + EPS)
            out = d.T * subln_ref[...] * (1.0 - LAM_INIT)
            col0 = MLA_HEADS * MLA_V_DIM + (u - MLA_HEADS // 2) * LANES
        o_ref[0, :, col0:col0 + LANES] = out.astype(BF16)


def _run_attn(qa, ka, vt, lam_vecs, subln, expert_weights):
    B, _, S, _ = qa.shape
    n_q = S // TQ
    n_slots = 2 * N_UNITS
    flat = [w.reshape(-1, w.shape[-1]) for w in expert_weights]
    slab = lambda w: pl.BlockSpec((w.shape[0] // (B * n_q), w.shape[1]), lambda b, i: (b * n_q + i, 0))
    assert all(w.shape[0] % (B * n_q * 16) == 0 for w in flat)
    outs = pl.pallas_call(
        _attn_kernel,
        grid=(B, n_q),
        in_specs=[
            pl.BlockSpec((1, N_QK_BLOCKS, TQ, LANES), lambda b, i: (b, 0, i, 0)),
            pl.BlockSpec((1, N_QK_BLOCKS, S, LANES), lambda b, i: (b, 0, 0, 0)),
            pl.BlockSpec((1, N_V_BLOCKS, LANES, S), lambda b, i: (b, 0, 0, 0)),
            pl.BlockSpec((4, DIFF_HALF), lambda b, i: (0, 0)),
            pl.BlockSpec((1, DIFF_V_DIM), lambda b, i: (0, 0)),
        ] + [slab(w) for w in flat],
        out_specs=[pl.BlockSpec((1, TQ, D_MODEL), lambda b, i: (b, i, 0))] + [slab(w) for w in flat],
        out_shape=[jax.ShapeDtypeStruct((B, S, D_MODEL), BF16)]
                  + [jax.ShapeDtypeStruct(w.shape, BF16) for w in flat],
        scratch_shapes=[
            pltpu.VMEM((n_slots, 1, TQ), F32),
            pltpu.VMEM((n_slots, 1, TQ), F32),
            pltpu.VMEM((n_slots, LANES, TQ), F32),
        ],
        compiler_params=pltpu.CompilerParams(
            dimension_semantics=("arbitrary", "arbitrary"), vmem_limit_bytes=VMEM_LIMIT),
        name="attn",
    )(qa, ka, vt, lam_vecs, subln, *flat)
    return outs[0], [w16.reshape(w.shape) for w16, w in zip(outs[1:], expert_weights)]


def _post_kernel(n_tok, o_ref, x_ref, w_o_ref, g_ffn_ref, w_r_ref, b_r_ref, tri_ref,
                 h_ref, tp_ref, key_ref, cnt_ref):
    tm = x_ref.shape[0]

    @pl.when(pl.program_id(0) == 0)
    def _():
        cnt_ref[...] = jnp.zeros(cnt_ref.shape, F32)

    tp_rows = tm // POST_SPLIT
    parts = [slice(p * tp_rows, (p + 1) * tp_rows) for p in range(POST_SPLIT)]
    ts = []
    for rows in parts:
        h = x_ref[rows] + jnp.dot(o_ref[rows], w_o_ref[...], preferred_element_type=F32)
        h_ref[rows] = h
        ts.append(_rms(h, g_ffn_ref[...]))
    logits = [jnp.dot(t.astype(BF16), w_r_ref[...], preferred_element_type=F32) + b_r_ref[...] for t in ts]
    routed = [_route_class(lg) for lg in logits]

    seen = cnt_ref[...]
    for rows, (cls, onehot) in zip(parts, routed):
        before = jnp.dot(tri_ref[...], onehot.astype(BF16), preferred_element_type=F32) + seen
        rank = jnp.sum(before * onehot, axis=1, keepdims=True)
        seen = seen + jnp.sum(onehot, axis=0, keepdims=True)
        lane = lax.broadcasted_iota(jnp.int32, onehot.shape, 1)
        key_cols = jnp.where(lane == 0, cls * float(n_tok) + rank, 0.0)
        key_ref[:, rows] = key_cols.T[:SUBLANES]
    cnt_ref[...] = seen

    for rows, t in zip(parts, ts):
        for c in range(SUBLANES):
            tp_ref[pl.ds(rows.start * SUBLANES + c, tp_rows, stride=SUBLANES), :] = t[:, c * LANES:(c + 1) * LANES]


def _route_class(lg):
    lane = lax.broadcasted_iota(jnp.int32, lg.shape, 1)
    lane_f = lane.astype(F32)
    first_lane = lambda hit: jnp.min(jnp.where(hit, lane_f, float(LANES)), axis=1, keepdims=True)

    is_group = lane < N_GROUPS
    gl = jnp.where(is_group, lg, NEG_BIG)
    gmax = jnp.max(gl, axis=1, keepdims=True)
    gidx = first_lane(gl == gmax)

    lo = N_GROUPS + EXPERTS_PER_GROUP * gidx
    el = jnp.where((lane_f >= lo) & (lane_f < lo + EXPERTS_PER_GROUP), lg, NEG_BIG)
    m1 = jnp.max(el, axis=1, keepdims=True)
    i1 = first_lane(el == m1)
    el2 = jnp.where(lane_f == i1, NEG_BIG, el)
    m2 = jnp.max(el2, axis=1, keepdims=True)
    i2 = first_lane(el2 == m2)

    a = jnp.minimum(i1, i2) - lo
    b = jnp.maximum(i1, i2) - lo
    pair = a * (2 * EXPERTS_PER_GROUP - 1 - a) * 0.5 + (b - a - 1.0)
    cls = gidx * PAIRS_PER_GROUP + pair
    return cls, jnp.where(lane_f == cls, 1.0, 0.0)


def _run_post(o2, x2, w_o, g_ffn, w_r, b_r, tri):
    T = x2.shape[0]
    tm = TM_POST
    full = lambda shape: pl.BlockSpec(shape, lambda i: (0,) * len(shape))
    rows = lambda w: pl.BlockSpec((tm, w), lambda i: (i, 0))
    return pl.pallas_call(
        functools.partial(_post_kernel, T),
        grid=(T // tm,),
        in_specs=[rows(D_MODEL), rows(D_MODEL), full((D_MODEL, D_MODEL)), full((1, D_MODEL)),
                  full((D_MODEL, LANES)), full((1, LANES)), full((tm // POST_SPLIT, tm // POST_SPLIT))],
        out_specs=[rows(D_MODEL), pl.BlockSpec((tm * SUBLANES, LANES), lambda i: (i, 0)),
                   pl.BlockSpec((SUBLANES, tm), lambda i: (0, i)), full((1, LANES))],
        out_shape=[
            jax.ShapeDtypeStruct((T, D_MODEL), F32),
            jax.ShapeDtypeStruct((T * SUBLANES, LANES), F32),
            jax.ShapeDtypeStruct((SUBLANES, T), F32),
            jax.ShapeDtypeStruct((1, LANES), F32),
        ],
        compiler_params=pltpu.CompilerParams(
            dimension_semantics=("arbitrary",), vmem_limit_bytes=VMEM_LIMIT),
        name="post",
    )(o2, x2, w_o, g_ffn, w_r, b_r, tri)


TILE_ROWS = TM_MOE * SUBLANES


TILES_PER_STEP = 4
GATHER_AHEAD = 2
GATHER_SLOTS = GATHER_AHEAD + 1


def _sorted_row(key_ref, off_ref, t):
    n_tok = key_ref.shape[0]
    key = key_ref[t]
    return off_ref[key >> (n_tok.bit_length() - 1)] + (key & (n_tok - 1))


def _moe_kernel(key_ref, off_ref, ta_ref, tb_ref, tp_hbm, w_r_ref, b_r_ref, wg_ref, wu_ref, wd_ref,
                y_ref, src_ref, xbuf, sem):
    step = pl.program_id(0)
    n_tok = key_ref.shape[0]
    n_used = off_ref[N_CLASSES] // TM_MOE

    def start_gather(tile):
        slot = tile % GATHER_SLOTS
        for r in range(TM_MOE):
            tok = src_ref[tile * TM_MOE + r]
            pltpu.make_async_copy(
                tp_hbm.at[pl.ds(pl.multiple_of(tok * SUBLANES, SUBLANES), SUBLANES), :],
                xbuf.at[pl.ds(pl.multiple_of(slot * TILE_ROWS + r * SUBLANES, SUBLANES), SUBLANES), :],
                sem.at[slot]).start(priority=r % 2)

    def wait_gather(tile):
        slot = tile % GATHER_SLOTS
        pltpu.make_async_copy(
            tp_hbm.at[pl.ds(0, TILE_ROWS), :],
            xbuf.at[pl.ds(pl.multiple_of(slot * TILE_ROWS, SUBLANES), TILE_ROWS), :],
            sem.at[slot]).wait()

    @pl.when(step == 0)
    def _():
        def fill(tile, c):
            for r in range(TM_MOE):
                src_ref[tile * TM_MOE + r] = (tile * TM_MOE + r) & (n_tok - 1)
            return c

        def invert(t, c):
            src_ref[_sorted_row(key_ref, off_ref, t)] = t
            return c

        lax.fori_loop(0, n_used + GATHER_AHEAD, fill, 0)
        lax.fori_loop(0, n_tok, invert, 0, unroll=16)
        for tile in range(GATHER_AHEAD):
            start_gather(tile)

    for sub in range(TILES_PER_STEP):
        _moe_tile(step * TILES_PER_STEP + sub, sub, n_used, ta_ref, tb_ref, w_r_ref, b_r_ref,
                  wg_ref, wu_ref, wd_ref, y_ref, xbuf, start_gather, wait_gather)


def _moe_tile(i, sub, n_used, ta_ref, tb_ref, w_r_ref, b_r_ref, wg_ref, wu_ref, wd_ref, y_ref, xbuf,
              start_gather, wait_gather):
    out0 = sub * TILE_ROWS

    @pl.when(i < n_used)
    def _():
        wait_gather(i)
        base = (i % GATHER_SLOTS) * TILE_ROWS
        x = jnp.concatenate([xbuf[pl.ds(base + c, TM_MOE, stride=SUBLANES), :] for c in range(SUBLANES)],
                            axis=1).astype(BF16)

        lg = jnp.dot(x, w_r_ref[...], preferred_element_type=F32) + b_r_ref[...]
        lane = lax.broadcasted_iota(jnp.int32, lg.shape, 1)
        pick = lambda idx: jnp.sum(jnp.where(lane == idx, lg, 0.0), axis=1, keepdims=True)
        ea, eb = ta_ref[i], tb_ref[i]
        g_logit = pick(ea // EXPERTS_PER_GROUP)
        p_group = 1.0 / jnp.sum(jnp.where(lane < N_GROUPS, jnp.exp(lg - g_logit), 0.0), axis=1, keepdims=True)
        l_a, l_b = pick(N_GROUPS + ea), pick(N_GROUPS + eb)
        gate_a = p_group / (1.0 + jnp.exp(l_b - l_a))
        gate_b = p_group / (1.0 + jnp.exp(l_a - l_b))

        def hidden(e_loc, gate):
            a = jnp.dot(x, wg_ref[e_loc], preferred_element_type=F32)
            u = jnp.dot(x, wu_ref[e_loc], preferred_element_type=F32)
            return ((a * jax.nn.sigmoid(a)) * u * gate).astype(BF16)

        a_loc, b_loc = ea % EXPERTS_PER_GROUP, eb % EXPERTS_PER_GROUP
        y = (jnp.dot(hidden(a_loc, gate_a), wd_ref[a_loc], preferred_element_type=F32)
             + jnp.dot(hidden(b_loc, gate_b), wd_ref[b_loc], preferred_element_type=F32))
        for c in range(SUBLANES):
            y_ref[pl.ds(out0 + c, TM_MOE, stride=SUBLANES), :] = y[:, c * LANES:(c + 1) * LANES]
        start_gather(i + GATHER_AHEAD)

    @pl.when(i >= n_used)
    def _():
        y_ref[pl.ds(out0, TILE_ROWS), :] = jnp.zeros((TILE_ROWS, LANES), F32)

        @pl.when(i < n_used + GATHER_AHEAD)
        def _():
            wait_gather(i)


def _run_moe(key, row_off, tile_a, tile_b, tp8, w_r, b_r, wg, wu, wd):
    n_tiles = tile_a.shape[0]
    n_tok = key.shape[0]
    assert n_tok & (n_tok - 1) == 0, "token count must be a power of two (class/rank key packing)"
    assert n_tiles % TILES_PER_STEP == 0
    group = lambda i, k, o, ta, tb: (ta[i * TILES_PER_STEP] // EXPERTS_PER_GROUP, 0, 0)
    grid_spec = pltpu.PrefetchScalarGridSpec(
        num_scalar_prefetch=4,
        grid=(n_tiles // TILES_PER_STEP,),
        in_specs=[
            pl.BlockSpec(memory_space=pl.ANY),
            pl.BlockSpec((D_MODEL, LANES), lambda i, k, o, ta, tb: (0, 0)),
            pl.BlockSpec((1, LANES), lambda i, k, o, ta, tb: (0, 0)),
            pl.BlockSpec((EXPERTS_PER_GROUP, D_MODEL, D_FF_EXPERT), group),
            pl.BlockSpec((EXPERTS_PER_GROUP, D_MODEL, D_FF_EXPERT), group),
            pl.BlockSpec((EXPERTS_PER_GROUP, D_FF_EXPERT, D_MODEL), group),
        ],
        out_specs=pl.BlockSpec((TILES_PER_STEP * TILE_ROWS, LANES), lambda i, k, o, ta, tb: (i, 0)),
        scratch_shapes=[
            pltpu.SMEM((n_tiles * TM_MOE,), jnp.int32),
            pltpu.VMEM((GATHER_SLOTS * TILE_ROWS, LANES), F32),
            pltpu.SemaphoreType.DMA((GATHER_SLOTS,)),
        ],
    )
    return pl.pallas_call(
        _moe_kernel,
        grid_spec=grid_spec,
        out_shape=jax.ShapeDtypeStruct((n_tiles * TILE_ROWS, LANES), F32),
        compiler_params=pltpu.CompilerParams(
            dimension_semantics=("arbitrary",), vmem_limit_bytes=VMEM_LIMIT),
        name="moe",
    )(key, row_off, tile_a, tile_b, tp8, w_r, b_r, wg, wu, wd)


FIN_ROWS = TM_FIN * SUBLANES


def _final_kernel(key_ref, off_ref, h_ref, y_hbm, g_ref, out_ref, ybuf, sem):
    i = pl.program_id(0)
    slot = i % 2

    def start_gather(tile, slot):
        def body(r2, c):
            for q in range(2):
                r = 2 * r2 + q
                p = _sorted_row(key_ref, off_ref, tile * TM_FIN + r)
                pltpu.make_async_copy(
                    y_hbm.at[pl.ds(pl.multiple_of(p * SUBLANES, SUBLANES), SUBLANES), :],
                    ybuf.at[pl.ds(pl.multiple_of(slot * FIN_ROWS + r * SUBLANES, SUBLANES), SUBLANES), :],
                    sem.at[slot]).start(priority=q)
            return c
        lax.fori_loop(0, TM_FIN // 2, body, 0, unroll=4)

    @pl.when(i == 0)
    def _():
        start_gather(0, 0)

    @pl.when(i + 1 < pl.num_programs(0))
    def _():
        start_gather(i + 1, 1 - slot)

    pltpu.make_async_copy(
        y_hbm.at[pl.ds(0, FIN_ROWS), :],
        ybuf.at[pl.ds(pl.multiple_of(slot * FIN_ROWS, SUBLANES), FIN_ROWS), :],
        sem.at[slot]).wait()
    base = slot * FIN_ROWS
    y = jnp.concatenate([ybuf[pl.ds(base + c, TM_FIN, stride=SUBLANES), :] for c in range(SUBLANES)], axis=1)
    out_ref[...] = _rms(h_ref[...] + y, g_ref[...])


def _run_final(key, row_off, h2, ys8, g_final):
    T = h2.shape[0]
    grid_spec = pltpu.PrefetchScalarGridSpec(
        num_scalar_prefetch=2,
        grid=(T // TM_FIN,),
        in_specs=[
            pl.BlockSpec((TM_FIN, D_MODEL), lambda i, k, o: (i, 0)),
            pl.BlockSpec(memory_space=pl.ANY),
            pl.BlockSpec((1, D_MODEL), lambda i, k, o: (0, 0)),
        ],
        out_specs=pl.BlockSpec((TM_FIN, D_MODEL), lambda i, k, o: (i, 0)),
        scratch_shapes=[pltpu.VMEM((2 * FIN_ROWS, LANES), F32), pltpu.SemaphoreType.DMA((2,))],
    )
    return pl.pallas_call(
        _final_kernel,
        grid_spec=grid_spec,
        out_shape=jax.ShapeDtypeStruct((T, D_MODEL), F32),
        compiler_params=pltpu.CompilerParams(
            dimension_semantics=("arbitrary",), vmem_limit_bytes=VMEM_LIMIT),
        name="final",
    )(key, row_off, h2, ys8, g_final)


def _prep_attention_weights(w_in, w_uq, w_ukv):
    d = w_in.shape[0]
    c0 = MLA_Q_RANK + MLA_KV_RANK
    kpe = w_in[:, c0:c0 + MLA_QK_ROPE]
    kpe_blk = jnp.concatenate(
        [jnp.zeros((d, MLA_QK_NOPE), F32), kpe, jnp.zeros((d, LANES - MLA_QK_NOPE - MLA_QK_ROPE), F32)], axis=1)
    w_in_r = jnp.concatenate([w_in[:, :c0], kpe_blk, w_in[:, c0 + MLA_QK_ROPE:]], axis=1).astype(BF16)

    dqk = MLA_QK_NOPE + MLA_QK_ROPE
    uq = w_uq.reshape(MLA_Q_RANK, MLA_HEADS, dqk)
    uq = jnp.pad(uq, ((0, 0), (0, 0), (0, LANES - dqk))).reshape(MLA_Q_RANK, MLA_HEADS * LANES).astype(BF16)

    ukv = w_ukv.reshape(MLA_KV_RANK, MLA_HEADS, MLA_QK_NOPE + MLA_V_DIM)
    uk = jnp.pad(ukv[:, :, :MLA_QK_NOPE], ((0, 0), (0, 0), (0, LANES - MLA_QK_NOPE)))
    uk = uk.reshape(MLA_KV_RANK, MLA_HEADS * LANES)
    uv = ukv[:, :, MLA_QK_NOPE:].reshape(MLA_KV_RANK, MLA_HEADS * MLA_V_DIM)
    w_ukv_r = jnp.concatenate([uk, uv], axis=1).astype(BF16)
    return w_in_r, uq, w_ukv_r


def _rope_freq_lanes():
    fb = ROPE_THETA ** (-jnp.arange(DIFF_HALF // 2, dtype=F32) / (DIFF_HALF // 2))
    fa = ROPE_THETA ** (-jnp.arange(MLA_QK_ROPE // 2, dtype=F32) / (MLA_QK_ROPE // 2))
    return jnp.concatenate([fb, fb, fa, fa, jnp.zeros((LANES - 96,), F32)])[None, :]


def kernel(x, positions, attn_norm_g, w_in, q_norm_g, w_uq, kv_norm_g, w_ukv, lambda_q1, lambda_k1, lambda_q2, lambda_k2, subln_g, w_o, ffn_norm_g, w_router_group, b_router_group, w_router_expert, b_router_expert, w_gate, w_up, w_down, final_norm_g):
    B, S, D = x.shape
    T = B * S
    l = 0

    w_in_r, w_uq_r, w_ukv_r = _prep_attention_weights(w_in[l], w_uq[l], w_ukv[l])
    pos_f = positions.astype(F32)[:, :, None]
    qa, ka, va = _run_proj(x, pos_f, _rope_freq_lanes(), attn_norm_g[l][None, :], w_in_r,
                           q_norm_g[l][None, :], w_uq_r, kv_norm_g[l][None, :], w_ukv_r)

    lam_vecs = jnp.stack([lambda_q1[l], lambda_k1[l], lambda_q2[l], lambda_k2[l]]).astype(F32)
    o, (wg16, wu16, wd16) = _run_attn(qa, ka, va, lam_vecs, subln_g[l][None, :],
                                      (w_gate[l], w_up[l], w_down[l]))

    w_r = jnp.concatenate(
        [w_router_group[l], w_router_expert[l].reshape(D, N_EXPERTS),
         jnp.zeros((D, LANES - N_GROUPS - N_EXPERTS), F32)], axis=1).astype(BF16)
    b_r = jnp.concatenate(
        [b_router_group[l], b_router_expert[l].reshape(N_EXPERTS),
         jnp.zeros((LANES - N_GROUPS - N_EXPERTS,), F32)])[None, :].astype(F32)
    ii = jnp.arange(TM_POST // POST_SPLIT)
    tri = (ii[None, :] < ii[:, None]).astype(BF16)
    x2 = x.reshape(T, D)
    h2, tp8, route, counts = _run_post(o.reshape(T, D), x2, w_o[l].astype(BF16), ffn_norm_g[l][None, :],
                                       w_r, b_r, tri)

    pad_tiles = N_GROUPS * (TILES_PER_STEP - 1)
    n_tiles_max = T // TM_MOE + N_CLASSES - 1 + pad_tiles + GATHER_AHEAD
    n_tiles_max += -n_tiles_max % TILES_PER_STEP
    key = route[0].astype(jnp.int32)
    cnt = counts[0, :N_CLASSES].astype(jnp.int32)
    tiles_gc = ((cnt + TM_MOE - 1) // TM_MOE).reshape(N_GROUPS, PAIRS_PER_GROUP)
    group_pad = -jnp.sum(tiles_gc, axis=1) % TILES_PER_STEP
    tiles_c = tiles_gc.at[:, -1].add(group_pad).reshape(N_CLASSES)
    tile_end = jnp.cumsum(tiles_c)
    n_tiles = tile_end[-1]
    row_off = jnp.concatenate([(tile_end - tiles_c) * TM_MOE, (n_tiles * TM_MOE)[None],
                               jnp.zeros((LANES - N_CLASSES - 1,), jnp.int32)])
    tile_ids = jnp.arange(n_tiles_max, dtype=jnp.int32)
    tile_cls = jnp.sum(jnp.minimum(tile_ids, n_tiles - 1)[:, None] >= tile_end[None, :], axis=1, dtype=jnp.int32)
    pair_a, pair_b = [], []
    for g in range(N_GROUPS):
        for a in range(EXPERTS_PER_GROUP):
            for b in range(a + 1, EXPERTS_PER_GROUP):
                pair_a.append(g * EXPERTS_PER_GROUP + a)
                pair_b.append(g * EXPERTS_PER_GROUP + b)
    cls_onehot = (tile_cls[:, None] == jnp.arange(N_CLASSES, dtype=jnp.int32)[None, :]).astype(jnp.int32)
    tile_a = cls_onehot @ jnp.array(pair_a, jnp.int32)
    tile_b = cls_onehot @ jnp.array(pair_b, jnp.int32)

    ys8 = _run_moe(key, row_off, tile_a, tile_b, tp8, w_r, b_r, wg16, wu16, wd16)
    out = _run_final(key, row_off, h2, ys8, final_norm_g[None, :])
    return out.reshape(B, S, D)
```
